```python
import math
import jax, jax.numpy as jnp
from jax import lax
import numpy as np

D_MODEL = 1024
BATCH = 32
SEQ = 2048
DEPTH = 2

HEAD_DIM = 64
A_HEADS = 8
B_HEADS = 8
A_WIDTH = A_HEADS * HEAD_DIM
B_WIDTH = B_HEADS * HEAD_DIM
DECAY_LORA = 64
ICLR_LORA = 64
GATE_LORA = 160
A_COLS = 3 * A_WIDTH + DECAY_LORA + ICLR_LORA + GATE_LORA
B_COLS = 3 * B_WIDTH
EVEN_COLS = A_COLS + B_COLS
SB_BLOCK = 128
GN_EPS = 64e-5
LRU_WIDTH = 1280
LRU_BLOCKS = 10
LRU_BLOCK_DIM = LRU_WIDTH // LRU_BLOCKS
CONV_WIDTH = 4
LRU_C = 8.0
N_GROUPS = 4
EXPERTS_PER_GROUP = 4
N_EXPERTS = N_GROUPS * EXPERTS_PER_GROUP
EXPERT_TOPK = 2
D_EXPERT = 512
RMS_EPS = 1e-6
N_EVEN = (DEPTH + 1) // 2
N_ODD = DEPTH // 2

kernel_name = 'hybrid_rwkv7_stickbreak_rglru_hmoe'


def rmsnorm(x, g):
    xf = x.astype(jnp.float32)
    y = xf * lax.rsqrt(jnp.mean(xf * xf, axis=-1, keepdims=True) + RMS_EPS)
    return (y * g.astype(jnp.float32)).astype(x.dtype)


def token_shift(p, mu):
    prev = jnp.pad(p, ((0, 0), (1, 0), (0, 0)))[:, :-1]
    return p + mu * (prev - p)


def rwkv7_time_mix(p, w0, decay_up, a0, iclr_up, gate_up, k_k, k_a, r_k, gn_w, gn_b):
    f32 = jnp.float32
    bsz, t_len, _ = p.shape
    o1, o2, o3 = A_WIDTH, 2 * A_WIDTH, 3 * A_WIDTH
    r, k, v = p[..., :o1], p[..., o1:o2], p[..., o2:o3]
    xw = p[..., o3:o3 + DECAY_LORA]
    xa = p[..., o3 + DECAY_LORA:o3 + DECAY_LORA + ICLR_LORA]
    xg = p[..., o3 + DECAY_LORA + ICLR_LORA:]
    w_log = -jax.nn.softplus(-(w0 + jnp.tanh(xw) @ decay_up).astype(f32)) - 0.5
    decay = jnp.exp(-jnp.exp(w_log))
    a = jax.nn.sigmoid((a0 + xa @ iclr_up).astype(f32))
    g = (jax.nn.sigmoid(xg) @ gate_up).astype(f32)

    def heads(t):
        return t.astype(f32).reshape(bsz, t_len, A_HEADS, HEAD_DIM)

    kk = heads(k * k_k)
    kk = kk / jnp.maximum(jnp.sqrt(jnp.sum(kk * kk, axis=-1, keepdims=True)), 1e-12)
    k_mod = k.astype(f32) * (1.0 + (a - 1.0) * k_a.astype(f32))
    r_h, k_h, v_h, w_h, a_h = heads(r), heads(k_mod), heads(v), heads(decay), heads(a)
    a_vec = -kk
    b_vec = kk * a_h

    def step(S, inp):
        r_t, w_t, k_t, v_t, a_t, b_t = inp
        sa = jnp.einsum('bhvk,bhk->bhv', S, a_t)
        S = S * w_t[:, :, None, :] + sa[..., None] * b_t[:, :, None, :] + v_t[..., None] * k_t[:, :, None, :]
        return S, jnp.einsum('bhvk,bhk->bhv', S, r_t)

    s0 = jnp.zeros((bsz, A_HEADS, HEAD_DIM, HEAD_DIM), f32)
    xs = tuple(jnp.swapaxes(t, 0, 1) for t in (r_h, w_h, k_h, v_h, a_vec, b_vec))
    _, y = lax.scan(step, s0, xs)
    y = jnp.swapaxes(y, 0, 1)
    mean = jnp.mean(y, axis=-1, keepdims=True)
    var = jnp.mean(jnp.square(y - mean), axis=-1, keepdims=True)
    y = ((y - mean) * lax.rsqrt(var + GN_EPS)).reshape(bsz, t_len, A_WIDTH)
    y = y * gn_w.astype(f32) + gn_b.astype(f32)
    bonus = jnp.sum(r_h * k_h * r_k.astype(f32), axis=-1, keepdims=True) * v_h
    y = y + bonus.reshape(bsz, t_len, A_WIDTH)
    return (y * g).astype(p.dtype)


def stick_breaking_attention(p, q_norm_g, k_norm_g):
    bsz, t_len, _ = p.shape

    def heads(t):
        return t.reshape(bsz, t_len, B_HEADS, HEAD_DIM)

    q = rmsnorm(heads(p[..., :B_WIDTH]), q_norm_g)
    k = rmsnorm(heads(p[..., B_WIDTH:2 * B_WIDTH]), k_norm_g)
    v = heads(p[..., 2 * B_WIDTH:])
    q, k, v = (jnp.swapaxes(t, 1, 2) for t in (q, k, v))
    scale = 1.0 / math.sqrt(HEAD_DIM)
    outs = []
    for blk in range(t_len // SB_BLOCK):
        q0 = blk * SB_BLOCK
        end = q0 + SB_BLOCK
        z = jnp.einsum('bhqd,bhkd->bhqk', q[:, :, q0:end], k[:, :, :end]).astype(jnp.float32) * scale
        t_idx = q0 + jnp.arange(SB_BLOCK)[:, None]
        s_idx = jnp.arange(end)[None, :]
        causal = s_idx < t_idx
        log_1m_beta = jnp.where(causal, jax.nn.log_sigmoid(-z), 0.0)
        after = lax.cumsum(log_1m_beta, axis=3, reverse=True) - log_1m_beta
        att = jnp.where(causal, jnp.exp(jax.nn.log_sigmoid(z) + after), 0.0)
        outs.append(jnp.einsum('bhqk,bhkd->bhqd', att.astype(v.dtype), v[:, :, :end]))
    o = jnp.concatenate(outs, axis=2)
    return jnp.swapaxes(o, 1, 2).reshape(bsz, t_len, B_WIDTH)


def causal_depthwise_conv(u, conv_w, conv_b):
    out = lax.conv_general_dilated(
        u, conv_w[:, None, :], window_strides=(1,), padding=((CONV_WIDTH - 1, 0),),
        dimension_numbers=('NWC', 'WIO', 'NWC'), feature_group_count=u.shape[-1])
    return out + conv_b


def rg_lru(u, w_rgate, b_rgate, w_igate, b_igate, lru_lambda):
    bsz, t_len, _ = u.shape
    ub = u.reshape(bsz, t_len, LRU_BLOCKS, LRU_BLOCK_DIM)
    r = jax.nn.sigmoid((jnp.einsum('btgi,gij->btgj', ub, w_rgate).reshape(bsz, t_len, LRU_WIDTH) + b_rgate).astype(jnp.float32))
    i = jax.nn.sigmoid((jnp.einsum('btgi,gij->btgj', ub, w_igate).reshape(bsz, t_len, LRU_WIDTH) + b_igate).astype(jnp.float32))
    log_a = -LRU_C * r * jax.nn.softplus(-lru_lambda.astype(jnp.float32))
    a = jnp.exp(log_a)
    b = jnp.sqrt(-jnp.expm1(2.0 * log_a)) * (i * u.astype(jnp.float32))

    def combine(c1, c2):
        a1, b1 = c1
        a2, b2 = c2
        return a1 * a2, a2 * b1 + b2

    _, h = lax.associative_scan(combine, (a, b), axis=1)
    return h.astype(u.dtype)


def hier_moe(h, w_group, b_group, w_erouter, b_erouter, exp_w_gate, exp_w_up, exp_w_down):
    f32 = jnp.float32
    bsz, t_len, d = h.shape
    hf = h.reshape(bsz * t_len, d)
    g_logits = (hf @ w_group).astype(f32) + b_group.astype(f32)
    g_prob = jax.nn.softmax(g_logits, axis=-1)
    g_top, g_idx = lax.top_k(g_prob, 1)
    e_logits = ((hf @ w_erouter).astype(f32) + b_erouter.astype(f32)).reshape(-1, N_GROUPS, EXPERTS_PER_GROUP)
    e_sel = jnp.take_along_axis(e_logits, g_idx[:, :, None], axis=1)[:, 0]
    e_prob = jax.nn.softmax(e_sel, axis=-1)
    e_top, e_idx = lax.top_k(e_prob, EXPERT_TOPK)
    gates = g_top * e_top / jnp.sum(e_top, axis=-1, keepdims=True)
    expert_id = g_idx * EXPERTS_PER_GROUP + e_idx
    dense_gate = jnp.sum(jax.nn.one_hot(expert_id, N_EXPERTS, dtype=f32) * gates[..., None], axis=1)
    dense_gate = dense_gate.astype(hf.dtype)
    y = jnp.zeros_like(hf)
    for e in range(N_EXPERTS):
        hid = jax.nn.silu(hf @ exp_w_gate[e]) * (hf @ exp_w_up[e])
        y = y + dense_gate[:, e:e + 1] * (hid @ exp_w_down[e])
    return y.reshape(bsz, t_len, d)


def setup_inputs(seed: int = 0) -> dict:
    key = jax.random.key(seed)
    ks = iter(jax.random.split(key, 48))
    f32 = jnp.float32

    def nrm(shape, scale):
        return jax.random.normal(next(ks), shape, f32) * scale

    def uni(shape, lo, hi):
        return jax.random.uniform(next(ks), shape, f32, minval=lo, maxval=hi)

    x = nrm((BATCH, SEQ, D_MODEL), 1.0)
    norm_mix = 1.0 + nrm((DEPTH, D_MODEL), 0.05)
    norm_ffn = 1.0 + nrm((DEPTH, D_MODEL), 0.05)
    w_in_even = nrm((N_EVEN, D_MODEL, EVEN_COLS), D_MODEL ** -0.5)
    mu_a = uni((N_EVEN, A_COLS), 0.0, 1.0)
    w0 = uni((N_EVEN, A_WIDTH), -6.0, -1.0)
    decay_up = nrm((N_EVEN, DECAY_LORA, A_WIDTH), 0.5 * DECAY_LORA ** -0.5)
    a0 = nrm((N_EVEN, A_WIDTH), 0.1)
    iclr_up = nrm((N_EVEN, ICLR_LORA, A_WIDTH), ICLR_LORA ** -0.5)
    gate_up = nrm((N_EVEN, GATE_LORA, A_WIDTH), GATE_LORA ** -0.5)
    k_k = 0.85 + nrm((N_EVEN, A_WIDTH), 0.05)
    k_a = 1.0 + nrm((N_EVEN, A_WIDTH), 0.05)
    r_k = nrm((N_EVEN, A_HEADS, HEAD_DIM), 0.1)
    gn_w = 1.0 + nrm((N_EVEN, A_WIDTH), 0.05)
    gn_b = nrm((N_EVEN, A_WIDTH), 0.02)
    q_norm_g = 1.0 + nrm((N_EVEN, HEAD_DIM), 0.05)
    k_norm_g = 1.0 + nrm((N_EVEN, HEAD_DIM), 0.05)
    w_out_even = nrm((N_EVEN, A_WIDTH + B_WIDTH, D_MODEL), (A_WIDTH + B_WIDTH) ** -0.5)
    w_in_odd = nrm((N_ODD, D_MODEL, 2 * LRU_WIDTH), D_MODEL ** -0.5)
    conv_w = nrm((N_ODD, CONV_WIDTH, LRU_WIDTH), CONV_WIDTH ** -0.5)
    conv_b = nrm((N_ODD, LRU_WIDTH), 0.02)
    w_rgate = nrm((N_ODD, LRU_BLOCKS, LRU_BLOCK_DIM, LRU_BLOCK_DIM), LRU_BLOCK_DIM ** -0.5)
    b_rgate = nrm((N_ODD, LRU_WIDTH), 0.02)
    w_igate = nrm((N_ODD, LRU_BLOCKS, LRU_BLOCK_DIM, LRU_BLOCK_DIM), LRU_BLOCK_DIM ** -0.5)
    b_igate = nrm((N_ODD, LRU_WIDTH), 0.02)
    a_pow_c = uni((N_ODD, LRU_WIDTH), 0.9, 0.999)
    s = a_pow_c ** (1.0 / LRU_C)
    lru_lambda = jnp.log(s) - jnp.log1p(-s)
    w_out_odd = nrm((N_ODD, LRU_WIDTH, D_MODEL), LRU_WIDTH ** -0.5)
    w_group = nrm((DEPTH, D_MODEL, N_GROUPS), D_MODEL ** -0.5)
    b_group = nrm((DEPTH, N_GROUPS), 0.01)
    w_erouter = nrm((DEPTH, D_MODEL, N_EXPERTS), D_MODEL ** -0.5)
    b_erouter = nrm((DEPTH, N_EXPERTS), 0.01)
    exp_w_gate = nrm((DEPTH, N_EXPERTS, D_MODEL, D_EXPERT), D_MODEL ** -0.5)
    exp_w_up = nrm((DEPTH, N_EXPERTS, D_MODEL, D_EXPERT), D_MODEL ** -0.5)
    exp_w_down = nrm((DEPTH, N_EXPERTS, D_EXPERT, D_MODEL), D_EXPERT ** -0.5)
    return {'x': x, 'norm_mix': norm_mix, 'norm_ffn': norm_ffn,
            'w_in_even': w_in_even, 'mu_a': mu_a, 'w0': w0, 'decay_up': decay_up,
            'a0': a0, 'iclr_up': iclr_up, 'gate_up': gate_up, 'k_k': k_k, 'k_a': k_a,
            'r_k': r_k, 'gn_w': gn_w, 'gn_b': gn_b, 'q_norm_g': q_norm_g, 'k_norm_g': k_norm_g,
            'w_out_even': w_out_even, 'w_in_odd': w_in_odd, 'conv_w': conv_w, 'conv_b': conv_b,
            'w_rgate': w_rgate, 'b_rgate': b_rgate, 'w_igate': w_igate, 'b_igate': b_igate,
            'lru_lambda': lru_lambda, 'w_out_odd': w_out_odd, 'w_group': w_group,
            'b_group': b_group, 'w_erouter': w_erouter, 'b_erouter': b_erouter,
            'exp_w_gate': exp_w_gate, 'exp_w_up': exp_w_up, 'exp_w_down': exp_w_down}


def reference(x, norm_mix, norm_ffn, w_in_even, mu_a, w0, decay_up, a0, iclr_up, gate_up,
              k_k, k_a, r_k, gn_w, gn_b, q_norm_g, k_norm_g, w_out_even, w_in_odd, conv_w,
              conv_b, w_rgate, b_rgate, w_igate, b_igate, lru_lambda, w_out_odd, w_group,
              b_group, w_erouter, b_erouter, exp_w_gate, exp_w_up, exp_w_down):
    for layer in range(DEPTH):
        h = rmsnorm(x, norm_mix[layer])
        if layer % 2 == 0:
            i = layer // 2
            proj = h @ w_in_even[i]
            pa = token_shift(proj[..., :A_COLS], mu_a[i])
            ya = rwkv7_time_mix(pa, w0[i], decay_up[i], a0[i], iclr_up[i], gate_up[i],
                                k_k[i], k_a[i], r_k[i], gn_w[i], gn_b[i])
            yb = stick_breaking_attention(proj[..., A_COLS:], q_norm_g[i], k_norm_g[i])
            x = x + jnp.concatenate([ya, yb], axis=-1) @ w_out_even[i]
        else:
            j = layer // 2
            u = h @ w_in_odd[j]
            gate_branch = jax.nn.gelu(u[..., :LRU_WIDTH])
            rec = causal_depthwise_conv(u[..., LRU_WIDTH:], conv_w[j], conv_b[j])
            rec = rg_lru(rec, w_rgate[j], b_rgate[j], w_igate[j], b_igate[j], lru_lambda[j])
            x = x + (gate_branch * rec) @ w_out_odd[j]
        x = x + hier_moe(rmsnorm(x, norm_ffn[layer]), w_group[layer], b_group[layer],
                         w_erouter[layer], b_erouter[layer], exp_w_gate[layer],
                         exp_w_up[layer], exp_w_down[layer])
    return x
```

```python
import functools
import math

import numpy as np
import jax
import jax.numpy as jnp
from jax import lax
from jax.experimental import pallas as pl
from jax.experimental.pallas import tpu as pltpu

F32 = jnp.float32
BF16 = jnp.bfloat16

V7X_LANES = 128
V7X_SUBLANES = 8
V7X_VMEM_LIMIT_BYTES = 48 * 1024 * 1024

HEAD_DIM = 64
RMS_EPS = 1e-6
GN_EPS = 64e-5
LRU_C = 8.0
N_GROUPS = 4
EXPERTS_PER_GROUP = 4
N_EXPERTS = 16
RWKV_CHUNK = 128
SB_BLOCK = 128
MOE_TILE = 512
NEG_BIG = -1e30


def _cparams(sem):
    return pltpu.CompilerParams(dimension_semantics=sem, vmem_limit_bytes=V7X_VMEM_LIMIT_BYTES)


def _mm(a, b):
    return jnp.dot(a.astype(BF16), b.astype(BF16), preferred_element_type=F32)


def _mm_nt(a, b):
    return lax.dot_general(a.astype(BF16), b.astype(BF16), (((1,), (1,)), ((), ())),
                           preferred_element_type=F32)


def _split_hilo(x):
    hi = x.astype(BF16)
    lo = (x - hi.astype(F32)).astype(BF16)
    return hi, lo


def _mm_hilo(x, m):
    hi, lo = _split_hilo(x)
    return (jnp.dot(hi, m, preferred_element_type=F32) + jnp.dot(lo, m, preferred_element_type=F32))


def _hilo_mm(m, x):
    hi, lo = _split_hilo(x)
    return (jnp.dot(m, hi, preferred_element_type=F32) + jnp.dot(m, lo, preferred_element_type=F32))


def _sigmoid(x):
    return 1.0 / (1.0 + jnp.exp(-x))


def _softplus(x):
    return jnp.maximum(x, 0.0) + jnp.log1p(jnp.exp(-jnp.abs(x)))


def _norm_matmul_body(x_ref, g_ref, w_ref, *out_refs, col_splits):
    x = x_ref[...]
    ms = jnp.mean(x * x, axis=-1, keepdims=True)
    h = (x * lax.rsqrt(ms + RMS_EPS) * g_ref[...]).astype(BF16)
    off = 0
    for o_ref, c in zip(out_refs, col_splits):
        o_ref[...] = jnp.dot(h, w_ref[:, off:off + c], preferred_element_type=F32).astype(o_ref.dtype)
        off += c


def _norm_matmul(x, g, w, col_splits, tm):
    n, d = x.shape
    ctot = w.shape[1]
    assert sum(col_splits) == ctot and n % tm == 0
    return pl.pallas_call(
        functools.partial(_norm_matmul_body, col_splits=tuple(col_splits)),
        grid=(n // tm,),
        in_specs=[pl.BlockSpec((tm, d), lambda i: (i, 0)),
                  pl.BlockSpec((1, d), lambda i: (0, 0)),
                  pl.BlockSpec((d, ctot), lambda i: (0, 0))],
        out_specs=[pl.BlockSpec((tm, c), lambda i: (i, 0)) for c in col_splits],
        out_shape=[jax.ShapeDtypeStruct((n, c), BF16) for c in col_splits],
        compiler_params=_cparams(("parallel",)),
    )(x, g, w)


def _proj_residual_body(*refs, n_in):
    x_ref = refs[0]
    a_refs = refs[1:1 + n_in]
    w_refs = refs[1 + n_in:1 + 2 * n_in]
    o_ref = refs[1 + 2 * n_in]
    acc = x_ref[...]
    for a_ref, w_ref in zip(a_refs, w_refs):
        acc = acc + jnp.dot(a_ref[...], w_ref[...], preferred_element_type=F32)
    o_ref[...] = acc


def _proj_residual(x, acts, ws, tm):
    n, d = x.shape
    n_in = len(acts)
    in_specs = [pl.BlockSpec((tm, d), lambda i: (i, 0))]
    in_specs += [pl.BlockSpec((tm, a.shape[1]), lambda i: (i, 0)) for a in acts]
    in_specs += [pl.BlockSpec(w.shape, lambda i: (0, 0)) for w in ws]
    return pl.pallas_call(
        functools.partial(_proj_residual_body, n_in=n_in),
        grid=(n // tm,),
        in_specs=in_specs,
        out_specs=pl.BlockSpec((tm, d), lambda i: (i, 0)),
        out_shape=jax.ShapeDtypeStruct((n, d), F32),
        compiler_params=_cparams(("parallel",)),
    )(x, *acts, *ws)


def _tri_inverse(a_sl, row, col):
    eye = (row == col).astype(F32)
    n1 = jnp.where((row // 8) == (col // 8), a_sl, 0.0)
    n2 = _mm(n1, n1)
    ipn = eye + n1
    p = ipn + _mm(ipn, n2)
    n4 = _mm(n2, n2)
    x = p + _mm(p, n4)
    k = 8
    c = a_sl.shape[0]
    while k < c:
        e = jnp.where(((row // (2 * k)) == (col // (2 * k))) & ((row // k) > (col // k)), a_sl, 0.0)
        x = x + _mm(_mm(x, e), x)
        k *= 2
    return x


def _rwkv_body(r_ref, k_ref, v_ref, l_ref, p_ref, mul_ref, du_ref, iu_ref, gu_ref, bones_ref, lt_ref,
               o_ref, z_ref, carry_ref):
    c = RWKV_CHUNK
    aw = r_ref.shape[1]
    lw = l_ref.shape[1]
    n_pairs = aw // V7X_LANES

    @pl.when(pl.program_id(1) == 0)
    def _():
        z_ref[...] = jnp.zeros_like(z_ref)
        carry_ref[...] = jnp.zeros_like(carry_ref)

    row1 = lax.broadcasted_iota(jnp.int32, (c, 1), 0)

    def token_shift(raw, prev_last, mu):
        prev = jnp.where(row1 == 0, prev_last, pltpu.roll(raw, 1, 0))
        return raw + mu * (prev - raw)

    prm = p_ref[...]
    mu_r, mu_k, mu_v, w0, a0, k_k, k_a, r_k, gn_w, gn_b = (prm[i:i + 1] for i in range(10))

    r_raw = r_ref[...].astype(F32)
    k_raw = k_ref[...].astype(F32)
    v_raw = v_ref[...].astype(F32)
    l_raw = l_ref[...].astype(F32)
    r = token_shift(r_raw, carry_ref[0:1, 0:aw], mu_r)
    k = token_shift(k_raw, carry_ref[0:1, aw:2 * aw], mu_k)
    v = token_shift(v_raw, carry_ref[0:1, 2 * aw:3 * aw], mu_v)
    xl = token_shift(l_raw, carry_ref[0:1, 3 * aw:3 * aw + lw], mul_ref[0:1])
    carry_ref[0:1, 0:aw] = r_raw[c - 1:c]
    carry_ref[0:1, aw:2 * aw] = k_raw[c - 1:c]
    carry_ref[0:1, 2 * aw:3 * aw] = v_raw[c - 1:c]
    carry_ref[0:1, 3 * aw:3 * aw + lw] = l_raw[c - 1:c]

    l0 = xl[:, :V7X_LANES]
    l1 = xl[:, V7X_LANES:]
    dw = _mm(jnp.tanh(l0), du_ref[...])
    da = _mm(l0, iu_ref[...])
    g = _mm(_sigmoid(l1), gu_ref[...])
    w_log = -_softplus(-(w0 + dw)) - 0.5
    logw = -jnp.exp(w_log)
    a = _sigmoid(a0 + da)

    bones = bones_ref[...]
    kk = k * k_k
    ssq = _mm_hilo(kk * kk, bones)
    kk = kk / jnp.maximum(jnp.sqrt(ssq), 1e-12)
    k_mod = k * (1.0 + (a - 1.0) * k_a)
    a_vec = -kk
    b_vec = kk * a

    cum = _hilo_mm(lt_ref[...], logw)
    cmid = cum[c // 2 - 1:c // 2]
    cend = cum[c - 1:c]
    emid = jnp.exp(cmid)
    a_s = a_vec * jnp.exp(cum - logw - cmid)
    a_t = a_s * emid
    r_s = r * jnp.exp(cum - cmid)
    r_t = r_s * emid
    e_neg = jnp.exp(cmid - cum)
    b_s = b_vec * e_neg
    k_s = k_mod * e_neg
    e_end = jnp.exp(cend - cum)
    b_p = b_vec * e_end
    k_p = k_mod * e_end
    p_end = jnp.exp(cend)

    row = lax.broadcasted_iota(jnp.int32, (c, c), 0)
    col = lax.broadcasted_iota(jnp.int32, (c, c), 1)
    strict = col < row
    incl = col <= row
    eye = row == col
    same_head = (row // HEAD_DIM) == (col // HEAD_DIM)
    m0 = lax.broadcasted_iota(jnp.int32, (1, V7X_LANES), 1) < HEAD_DIM

    def stack(x):
        return jnp.concatenate([jnp.where(m0, x, 0.0), jnp.where(m0, 0.0, x)], axis=0)

    y_parts = []
    for p in range(n_pairs):
        sl = slice(p * V7X_LANES, (p + 1) * V7X_LANES)
        v_p = v[:, sl]
        lhs = jnp.concatenate([stack(a_s[:, sl]), stack(r_s[:, sl])], axis=0)
        rhs = jnp.concatenate([b_s[:, sl], k_s[:, sl]], axis=0)
        gm = _mm_nt(lhs, rhs)
        aab = [jnp.where(strict, gm[h * c:(h + 1) * c, 0:c], 0.0) for h in range(2)]
        aak = [jnp.where(strict, gm[h * c:(h + 1) * c, c:2 * c], 0.0) for h in range(2)]
        arb = [jnp.where(incl, gm[(2 + h) * c:(3 + h) * c, 0:c], 0.0) for h in range(2)]
        ark = [jnp.where(incl, gm[(2 + h) * c:(3 + h) * c, c:2 * c], 0.0) for h in range(2)]
        tinv = [_tri_inverse(aab[h], row, col) for h in range(2)]
        vs = stack(v_p)
        akv = _mm(jnp.concatenate(aak, axis=1), vs)
        wu = _mm(jnp.concatenate(tinv, axis=1),
                 jnp.concatenate([stack(a_t[:, sl]), stack(akv)], axis=1))
        w_t = wu[:, :V7X_LANES]
        u_v = wu[:, V7X_LANES:]
        arb_cat = jnp.concatenate(arb, axis=1)
        q_t = r_t[:, sl] + _mm(arb_cat, stack(w_t))
        y_v = _mm(jnp.concatenate([arb_cat] + ark, axis=1),
                  jnp.concatenate([stack(u_v), vs], axis=0))
        bp_t = b_p[:, sl].T
        kp_t = k_p[:, sl].T
        m_mat = jnp.where(same_head, _mm(bp_t, w_t), 0.0)
        n_mat = jnp.where(same_head,
                          _mm(jnp.concatenate([bp_t, kp_t], axis=1),
                              jnp.concatenate([u_v, v_p], axis=0)), 0.0)
        z = z_ref[p]
        y_parts.append(_mm(q_t, z) + y_v)
        p_col = jnp.sum(jnp.where(eye, p_end[:, sl], 0.0), axis=1, keepdims=True)
        z_ref[p] = p_col * z + _mm(m_mat, z) + n_mat

    y = jnp.concatenate(y_parts, axis=1)
    inv_hd = 1.0 / HEAD_DIM
    mean = _mm_hilo(y, bones) * inv_hd
    d = y - mean
    var = _mm_hilo(d * d, bones) * inv_hd
    yn = d * lax.rsqrt(var + GN_EPS) * gn_w + gn_b
    bonus = _mm_hilo(r * k_mod * r_k, bones) * v
    o_ref[...] = ((yn + bonus) * g).astype(o_ref.dtype)


def _rwkv(r, k, v, l, prm, mul, du, iu, gu, bsz, t_len):
    n, aw = r.shape
    lw = l.shape[1]
    c = RWKV_CHUNK
    nt = t_len // c
    heads = np.arange(aw) // HEAD_DIM
    bones = jnp.asarray((heads[:, None] == heads[None, :]).astype(np.float32), BF16)
    lt = jnp.asarray(np.tril(np.ones((c, c), np.float32)), BF16)
    tok = lambda w: pl.BlockSpec((c, w), lambda b, t: (b * nt + t, 0))
    full = lambda arr: pl.BlockSpec(arr.shape, lambda b, t: (0,) * arr.ndim)
    return pl.pallas_call(
        _rwkv_body,
        grid=(bsz, nt),
        in_specs=[tok(aw), tok(aw), tok(aw), tok(lw), full(prm), full(mul), full(du), full(iu), full(gu),
                  full(bones), full(lt)],
        out_specs=tok(aw),
        out_shape=jax.ShapeDtypeStruct((n, aw), BF16),
        scratch_shapes=[pltpu.VMEM((aw // V7X_LANES, V7X_LANES, V7X_LANES), F32),
                        pltpu.VMEM((V7X_SUBLANES, 3 * aw + lw), F32)],
        compiler_params=_cparams(("parallel", "arbitrary")),
    )(r, k, v, l, prm, mul, du, iu, gu, bones, lt)


def _sb_body(q_ref, k_ref, v_ref, qg_ref, kg_ref, bo_ref, su_ref, o_ref, qm_ref, kn_ref, acc_ref, car_ref):
    t_len = q_ref.shape[0]
    blk = SB_BLOCK
    m0 = lax.broadcasted_iota(jnp.int32, (1, V7X_LANES), 1) < HEAD_DIM
    scale = 1.0 / math.sqrt(HEAD_DIM)
    inv_hd = 1.0 / HEAD_DIM

    def prep(i, _):
        s0 = pl.multiple_of(i * blk, blk)
        q = q_ref[pl.ds(s0, blk), :].astype(F32)
        qn = q * lax.rsqrt(_mm_hilo(q * q, bo_ref[...]) * inv_hd + RMS_EPS) * (qg_ref[...] * scale)
        qm_ref[0, pl.ds(s0, blk), :] = jnp.where(m0, qn, 0.0).astype(BF16)
        qm_ref[1, pl.ds(s0, blk), :] = jnp.where(m0, 0.0, qn).astype(BF16)
        kx = k_ref[pl.ds(s0, blk), :].astype(F32)
        kn = kx * lax.rsqrt(_mm_hilo(kx * kx, bo_ref[...]) * inv_hd + RMS_EPS) * kg_ref[...]
        kn_ref[pl.ds(s0, blk), :] = kn.astype(BF16)
        return 0

    lax.fori_loop(0, t_len // blk, prep, 0)

    row = lax.broadcasted_iota(jnp.int32, (blk, blk), 0)
    col = lax.broadcasted_iota(jnp.int32, (blk, blk), 1)
    causal = col < row

    def process(q0, k0, diag):
        kb = kn_ref[pl.ds(k0, blk), :]
        vb = v_ref[pl.ds(k0, blk), :]
        for h in range(2):
            z = _mm_nt(qm_ref[h, pl.ds(q0, blk), :], kb)
            lg = -_softplus(z)
            if diag:
                lg = jnp.where(causal, lg, 0.0)
            rs = _mm_hilo(lg, su_ref[...])
            logit = z + lg + rs[:, :blk] + car_ref[h]
            if diag:
                logit = jnp.where(causal, logit, NEG_BIG)
            att = jnp.exp(logit)
            acc_ref[h] += _mm(att, vb)
            car_ref[h] += rs[:, blk:]

    def qblock(i, _):
        q0 = pl.multiple_of(i * blk, blk)
        acc_ref[...] = jnp.zeros_like(acc_ref)
        car_ref[...] = jnp.zeros_like(car_ref)
        process(q0, q0, True)

        def kblock(jj, _):
            k0 = pl.multiple_of((i - 1 - jj) * blk, blk)
            process(q0, k0, False)
            return 0

        lax.fori_loop(0, i, kblock, 0)
        o_ref[pl.ds(q0, blk), :] = jnp.where(m0, acc_ref[0], acc_ref[1]).astype(o_ref.dtype)
        return 0

    lax.fori_loop(0, t_len // blk, qblock, 0)


def _stick_breaking(q, k, v, qg, kg, bsz, t_len):
    n, bw = q.shape
    n_pairs = bw // V7X_LANES
    blk = SB_BLOCK
    heads = np.arange(V7X_LANES) // HEAD_DIM
    bo = jnp.asarray((heads[:, None] == heads[None, :]).astype(np.float32), BF16)
    idx = np.arange(blk)
    su = np.concatenate([(idx[:, None] > idx[None, :]).astype(np.float32), np.ones((blk, blk), np.float32)], axis=1)
    su = jnp.asarray(su, BF16)
    tok = pl.BlockSpec((t_len, V7X_LANES), lambda b, p: (b, p))
    full = lambda arr: pl.BlockSpec(arr.shape, lambda b, p: (0,) * arr.ndim)
    return pl.pallas_call(
        _sb_body,
        grid=(bsz, n_pairs),
        in_specs=[tok, tok, tok, full(qg), full(kg), full(bo), full(su)],
        out_specs=tok,
        out_shape=jax.ShapeDtypeStruct((n, bw), BF16),
        scratch_shapes=[pltpu.VMEM((2, t_len, V7X_LANES), BF16),
                        pltpu.VMEM((t_len, V7X_LANES), BF16),
                        pltpu.VMEM((2, blk, V7X_LANES), F32),
                        pltpu.VMEM((2, blk, blk), F32)],
        compiler_params=_cparams(("parallel", "parallel")),
    )(q, k, v, qg, kg, bo, su)


def _lru_body(gate_ref, rec_ref, cw_ref, pv_ref, wr_ref, wi_ref, o_ref, prev_ref, h_ref):
    tt = rec_ref.shape[0]
    width = rec_ref.shape[1]
    n_blocks = width // V7X_LANES

    @pl.when(pl.program_id(1) == 0)
    def _():
        prev_ref[...] = jnp.zeros_like(prev_ref)
        h_ref[...] = jnp.zeros_like(h_ref)

    row = lax.broadcasted_iota(jnp.int32, (tt, 1), 0)
    row8 = lax.broadcasted_iota(jnp.int32, (V7X_SUBLANES, 1), 0)

    for gi in range(n_blocks):
        sl = slice(gi * V7X_LANES, (gi + 1) * V7X_LANES)
        u = rec_ref[:, sl].astype(F32)
        prev = prev_ref[:, sl]

        def shifted(s):
            ru = pltpu.roll(u, s, 0)
            first = jnp.where(row8 < s, pltpu.roll(prev, s, 0), ru[0:V7X_SUBLANES])
            return jnp.concatenate([first, ru[V7X_SUBLANES:]], axis=0)

        cw = cw_ref[:, sl]
        rc = (cw[3:4] * u + cw[2:3] * shifted(1) + cw[1:2] * shifted(2) + cw[0:1] * shifted(3)
              + pv_ref[0:1, sl])
        prev_ref[:, sl] = u[tt - V7X_SUBLANES:tt]

        rgate = _sigmoid(_mm(rc, wr_ref[gi]) + pv_ref[1:2, sl])
        igate = _sigmoid(_mm(rc, wi_ref[gi]) + pv_ref[2:3, sl])
        log_a = -LRU_C * rgate * _softplus(-pv_ref[3:4, sl])
        a = jnp.exp(log_a)
        b = jnp.sqrt(1.0 - jnp.exp(2.0 * log_a)) * (igate * rc)

        s = 1
        while s < tt:
            keep = row >= s
            a_sh = jnp.where(keep, pltpu.roll(a, s, 0), 1.0)
            b_sh = jnp.where(keep, pltpu.roll(b, s, 0), 0.0)
            b = a * b_sh + b
            a = a * a_sh
            s *= 2
        h = a * h_ref[0:1, sl] + b
        h_ref[0:1, sl] = h[tt - 1:tt]
        o_ref[:, sl] = (jax.nn.gelu(gate_ref[:, sl].astype(F32)) * h).astype(o_ref.dtype)


def _rg_lru(gate, rec, cw, pv, wr, wi, bsz, t_len, tt):
    n, width = rec.shape
    nt = t_len // tt
    tok = pl.BlockSpec((tt, width), lambda b, t: (b * nt + t, 0))
    full = lambda arr: pl.BlockSpec(arr.shape, lambda b, t: (0,) * arr.ndim)
    return pl.pallas_call(
        _lru_body,
        grid=(bsz, nt),
        in_specs=[tok, tok, full(cw), full(pv), full(wr), full(wi)],
        out_specs=tok,
        out_shape=jax.ShapeDtypeStruct((n, width), BF16),
        scratch_shapes=[pltpu.VMEM((V7X_SUBLANES, width), F32), pltpu.VMEM((V7X_SUBLANES, width), F32)],
        compiler_params=_cparams(("parallel", "arbitrary")),
    )(gate, rec, cw, pv, wr, wi)


def _norm_router_body(x_ref, g_ref, wh_ref, wl_ref, h_ref, lg_ref):
    x = x_ref[...]
    ms = jnp.mean(x * x, axis=-1, keepdims=True)
    h = x * lax.rsqrt(ms + RMS_EPS) * g_ref[...]
    h_hi, h_lo = _split_hilo(h)
    h_ref[...] = h_hi
    wh = wh_ref[...]
    lg_ref[...] = (jnp.dot(h_hi, wh, preferred_element_type=F32)
                   + jnp.dot(h_lo, wh, preferred_element_type=F32)
                   + jnp.dot(h_hi, wl_ref[...], preferred_element_type=F32))


def _norm_router(x, g, w_router, tm):
    n, d = x.shape
    wpad = jnp.zeros((d, V7X_LANES), F32).at[:, :w_router.shape[1]].set(w_router)
    wh = wpad.astype(BF16)
    wl = (wpad - wh.astype(F32)).astype(BF16)
    return pl.pallas_call(
        _norm_router_body,
        grid=(n // tm,),
        in_specs=[pl.BlockSpec((tm, d), lambda i: (i, 0)),
                  pl.BlockSpec((1, d), lambda i: (0, 0)),
                  pl.BlockSpec((d, V7X_LANES), lambda i: (0, 0)),
                  pl.BlockSpec((d, V7X_LANES), lambda i: (0, 0))],
        out_specs=[pl.BlockSpec((tm, d), lambda i: (i, 0)),
                   pl.BlockSpec((tm, V7X_LANES), lambda i: (i, 0))],
        out_shape=[jax.ShapeDtypeStruct((n, d), BF16), jax.ShapeDtypeStruct((n, V7X_LANES), F32)],
        compiler_params=_cparams(("parallel",)),
    )(x, g, wh, wl)


def _moe_body(te_ref, nv_ref, x_ref, wg_ref, wu_ref, wd_ref, o_ref):
    @pl.when(pl.program_id(0) < nv_ref[0])
    def _():
        x = x_ref[...]
        gt = jnp.dot(x, wg_ref[0], preferred_element_type=F32)
        up = jnp.dot(x, wu_ref[0], preferred_element_type=F32)
        hid = (gt * _sigmoid(gt)) * up
        o_ref[...] = jnp.dot(hid.astype(BF16), wd_ref[0], preferred_element_type=F32).astype(o_ref.dtype)


def _moe_grouped(xs, tile_expert, n_valid, wg, wu, wd):
    p_rows, d = xs.shape
    de = wg.shape[2]
    tm = MOE_TILE
    grid_spec = pltpu.PrefetchScalarGridSpec(
        num_scalar_prefetch=2,
        grid=(p_rows // tm,),
        in_specs=[pl.BlockSpec((tm, d), lambda i, te, nv: (i, 0)),
                  pl.BlockSpec((1, d, de), lambda i, te, nv: (te[i], 0, 0)),
                  pl.BlockSpec((1, d, de), lambda i, te, nv: (te[i], 0, 0)),
                  pl.BlockSpec((1, de, d), lambda i, te, nv: (te[i], 0, 0))],
        out_specs=pl.BlockSpec((tm, d), lambda i, te, nv: (i, 0)),
    )
    return pl.pallas_call(
        _moe_body,
        grid_spec=grid_spec,
        out_shape=jax.ShapeDtypeStruct((p_rows, d), BF16),
        compiler_params=_cparams(("arbitrary",)),
    )(tile_expert, n_valid, xs, wg, wu, wd)


def _hier_moe(x, g_norm, w_group, b_group, w_erouter, b_erouter, wg, wu, wd, tm):
    n, d = x.shape
    hn, logits = _norm_router(x, g_norm, jnp.concatenate([w_group, w_erouter], axis=1), tm)
    g_logits = logits[:, :N_GROUPS] + b_group
    g_prob = jax.nn.softmax(g_logits, axis=-1)
    g_top, g_idx = lax.top_k(g_prob, 1)
    e_logits = (logits[:, N_GROUPS:N_GROUPS + N_EXPERTS] + b_erouter).reshape(-1, N_GROUPS, EXPERTS_PER_GROUP)
    e_sel = jnp.take_along_axis(e_logits, g_idx[:, :, None], axis=1)[:, 0]
    e_prob = jax.nn.softmax(e_sel, axis=-1)
    e_top, e_idx = lax.top_k(e_prob, 2)
    gates = g_top * e_top / jnp.sum(e_top, axis=-1, keepdims=True)
    expert_id = (g_idx * EXPERTS_PER_GROUP + e_idx).astype(jnp.int32)

    tmm = MOE_TILE
    eid = expert_id.reshape(-1)
    onehot = (eid[:, None] == jnp.arange(N_EXPERTS, dtype=jnp.int32)[None, :]).astype(jnp.int32)
    csum = jnp.cumsum(onehot, axis=0)
    rank = jnp.take_along_axis(csum, eid[:, None], axis=1)[:, 0] - 1
    counts = csum[-1]
    padded = ((counts + tmm - 1) // tmm) * tmm
    ends = jnp.cumsum(padded)
    starts = ends - padded
    pos = starts[eid] + rank
    p_rows = 2 * n + N_EXPERTS * tmm
    n_tiles = p_rows // tmm
    row_token = jnp.zeros((p_rows,), jnp.int32).at[pos].set(jnp.arange(2 * n, dtype=jnp.int32) // 2)
    tile_expert = jnp.minimum(
        jnp.searchsorted(ends, jnp.arange(n_tiles, dtype=jnp.int32) * tmm, side="right"), N_EXPERTS - 1
    ).astype(jnp.int32)
    n_valid = (ends[-1] // tmm).astype(jnp.int32).reshape(1)

    xs = jnp.take(hn, row_token, axis=0)
    ys = _moe_grouped(xs, tile_expert, n_valid, wg, wu, wd)
    pos2 = pos.reshape(n, 2)
    y = (gates[:, 0:1] * jnp.take(ys, pos2[:, 0], axis=0).astype(F32)
         + gates[:, 1:2] * jnp.take(ys, pos2[:, 1], axis=0).astype(F32))
    return x + y


def _pad_rows(w, rows, offset=0):
    out = jnp.zeros((rows, w.shape[1]), w.dtype)
    return out.at[offset:offset + w.shape[0]].set(w)


def kernel(x, norm_mix, norm_ffn, w_in_even, mu_a, w0, decay_up, a0, iclr_up, gate_up, k_k, k_a, r_k, gn_w,
           gn_b, q_norm_g, k_norm_g, w_out_even, w_in_odd, conv_w, conv_b, w_rgate, b_rgate, w_igate,
           b_igate, lru_lambda, w_out_odd, w_group, b_group, w_erouter, b_erouter, exp_w_gate, exp_w_up,
           exp_w_down):
    bsz, t_len, d = x.shape
    n = bsz * t_len
    depth = norm_mix.shape[0]
    aw = decay_up.shape[2]
    d_lora, i_lora, g_lora = decay_up.shape[1], iclr_up.shape[1], gate_up.shape[1]
    bw = (w_in_even.shape[2] - (3 * aw + d_lora + i_lora + g_lora)) // 3
    lw = w_rgate.shape[1] * w_rgate.shape[2]
    assert d_lora + i_lora == V7X_LANES and g_lora <= 2 * V7X_LANES
    lora_w = 3 * V7X_LANES
    tm = 512
    xf = x.reshape(n, d)

    for layer in range(depth):
        gm = norm_mix[layer].reshape(1, d)
        if layer % 2 == 0:
            i = layer // 2
            w_in = w_in_even[i]
            o3 = 3 * aw
            a_cols = o3 + d_lora + i_lora + g_lora
            pad = jnp.zeros((d, lora_w - (d_lora + i_lora + g_lora)), F32)
            w_cat = jnp.concatenate([w_in[:, :a_cols], pad, w_in[:, a_cols:]], axis=1).astype(BF16)
            r, k, v, lo, qb, kb, vb = _norm_matmul(xf, gm, w_cat, (aw, aw, aw, lora_w, bw, bw, bw), tm)
            mu = mu_a[i]
            rows = [mu[:aw], mu[aw:2 * aw], mu[2 * aw:o3], w0[i], a0[i], k_k[i], k_a[i], r_k[i].reshape(-1),
                    gn_w[i], gn_b[i]]
            prm = jnp.zeros((16, aw), F32).at[:len(rows)].set(jnp.stack(rows))
            mul = jnp.zeros((V7X_SUBLANES, lora_w), F32).at[0, :a_cols - o3].set(mu[o3:])
            du = _pad_rows(decay_up[i], V7X_LANES, 0).astype(BF16)
            iu = _pad_rows(iclr_up[i], V7X_LANES, d_lora).astype(BF16)
            gu = _pad_rows(gate_up[i], 2 * V7X_LANES, 0).astype(BF16)
            ya = _rwkv(r, k, v, lo, prm, mul, du, iu, gu, bsz, t_len)
            qg = jnp.tile(q_norm_g[i], V7X_LANES // HEAD_DIM).reshape(1, V7X_LANES)
            kg = jnp.tile(k_norm_g[i], V7X_LANES // HEAD_DIM).reshape(1, V7X_LANES)
            yb = _stick_breaking(qb, kb, vb, qg, kg, bsz, t_len)
            w_out = w_out_even[i].astype(BF16)
            xf = _proj_residual(xf, [ya, yb], [w_out[:aw], w_out[aw:]], tm)
        else:
            j = layer // 2
            gate, rec = _norm_matmul(xf, gm, w_in_odd[j].astype(BF16), (lw, lw), tm)
            cw = jnp.zeros((V7X_SUBLANES, lw), F32).at[:conv_w.shape[1]].set(conv_w[j])
            pv = jnp.zeros((V7X_SUBLANES, lw), F32).at[:4].set(
                jnp.stack([conv_b[j], b_rgate[j], b_igate[j], lru_lambda[j]]))
            yl = _rg_lru(gate, rec, cw, pv, w_rgate[j].astype(BF16), w_igate[j].astype(BF16), bsz, t_len,
                         min(256, t_len))
            xf = _proj_residual(xf, [yl], [w_out_odd[j].astype(BF16)], tm)
        xf = _hier_moe(xf, norm_ffn[layer].reshape(1, d), w_group[layer], b_group[layer], w_erouter[layer],
                       b_erouter[layer], exp_w_gate[layer].astype(BF16), exp_w_up[layer].astype(BF16),
                       exp_w_down[layer].astype(BF16), tm)
    return xf.reshape(bsz, t_len, d)
```

```python
import functools
import math

import numpy as np
import jax
import jax.numpy as jnp
from jax import lax
from jax.experimental import pallas as pl
from jax.experimental.pallas import tpu as pltpu

F32 = jnp.float32
BF16 = jnp.bfloat16

V7X_LANES = 128
V7X_SUBLANES = 8
V7X_VMEM_LIMIT_BYTES = 48 * 1024 * 1024

HEAD_DIM = 64
RMS_EPS = 1e-6
GN_EPS = 64e-5
LRU_C = 8.0
N_GROUPS = 4
EXPERTS_PER_GROUP = 4
N_EXPERTS = 16
RWKV_CHUNK = 128
SB_BLOCK = 128
SB_KEY_GROUP = 4
MOE_TILE = 512
NEG_BIG = -1e30


def _cparams(sem):
    return pltpu.CompilerParams(dimension_semantics=sem, vmem_limit_bytes=V7X_VMEM_LIMIT_BYTES)


def _mm(a, b):
    return jnp.dot(a.astype(BF16), b.astype(BF16), preferred_element_type=F32)


def _mm_nt(a, b):
    return lax.dot_general(a.astype(BF16), b.astype(BF16), (((1,), (1,)), ((), ())),
                           preferred_element_type=F32)


def _split_hilo(x):
    hi = x.astype(BF16)
    lo = (x - hi.astype(F32)).astype(BF16)
    return hi, lo


def _mm_hilo(x, m):
    hi, lo = _split_hilo(x)
    return (jnp.dot(hi, m, preferred_element_type=F32) + jnp.dot(lo, m, preferred_element_type=F32))


def _hilo_mm(m, x):
    hi, lo = _split_hilo(x)
    return (jnp.dot(m, hi, preferred_element_type=F32) + jnp.dot(m, lo, preferred_element_type=F32))


def _sigmoid(x):
    return 1.0 / (1.0 + jnp.exp(-x))


def _softplus(x):
    return jnp.maximum(x, 0.0) + jnp.log(1.0 + jnp.exp(-jnp.abs(x)))


def _norm_matmul_body(x_ref, g_ref, w_ref, *out_refs, col_splits):
    x = x_ref[...]
    ms = jnp.mean(x * x, axis=-1, keepdims=True)
    h = (x * lax.rsqrt(ms + RMS_EPS) * g_ref[...]).astype(BF16)
    off = 0
    for o_ref, c in zip(out_refs, col_splits):
        o_ref[...] = jnp.dot(h, w_ref[:, off:off + c], preferred_element_type=F32).astype(o_ref.dtype)
        off += c


def _norm_matmul(x, g, w, col_splits, tm):
    n, d = x.shape
    ctot = w.shape[1]
    assert sum(col_splits) == ctot and n % tm == 0
    return pl.pallas_call(
        functools.partial(_norm_matmul_body, col_splits=tuple(col_splits)),
        grid=(n // tm,),
        in_specs=[pl.BlockSpec((tm, d), lambda i: (i, 0)),
                  pl.BlockSpec((1, d), lambda i: (0, 0)),
                  pl.BlockSpec((d, ctot), lambda i: (0, 0))],
        out_specs=[pl.BlockSpec((tm, c), lambda i: (i, 0)) for c in col_splits],
        out_shape=[jax.ShapeDtypeStruct((n, c), BF16) for c in col_splits],
        compiler_params=_cparams(("parallel",)),
    )(x, g, w)


def _proj_residual_body(*refs, n_in):
    x_ref = refs[0]
    a_refs = refs[1:1 + n_in]
    w_refs = refs[1 + n_in:1 + 2 * n_in]
    o_ref = refs[1 + 2 * n_in]
    acc = x_ref[...]
    for a_ref, w_ref in zip(a_refs, w_refs):
        acc = acc + jnp.dot(a_ref[...], w_ref[...], preferred_element_type=F32)
    o_ref[...] = acc


def _proj_residual(x, acts, ws, tm):
    n, d = x.shape
    n_in = len(acts)
    in_specs = [pl.BlockSpec((tm, d), lambda i: (i, 0))]
    in_specs += [pl.BlockSpec((tm, a.shape[1]), lambda i: (i, 0)) for a in acts]
    in_specs += [pl.BlockSpec(w.shape, lambda i: (0, 0)) for w in ws]
    return pl.pallas_call(
        functools.partial(_proj_residual_body, n_in=n_in),
        grid=(n // tm,),
        in_specs=in_specs,
        out_specs=pl.BlockSpec((tm, d), lambda i: (i, 0)),
        out_shape=jax.ShapeDtypeStruct((n, d), F32),
        compiler_params=_cparams(("parallel",)),
    )(x, *acts, *ws)


def _tri_inverse_many(mats, row, col):
    eye = (row == col).astype(F32)
    blk8 = (row // 8) == (col // 8)
    n1 = [jnp.where(blk8, a, 0.0) for a in mats]
    n2 = [_mm(n, n) for n in n1]
    ipn = [eye + n for n in n1]
    p = [i + _mm(i, m) for i, m in zip(ipn, n2)]
    n4 = [_mm(m, m) for m in n2]
    x = [q + _mm(q, m) for q, m in zip(p, n4)]
    k = 8
    c = mats[0].shape[0]
    while k < c:
        sel = ((row // (2 * k)) == (col // (2 * k))) & ((row // k) > (col // k))
        e = [jnp.where(sel, a, 0.0) for a in mats]
        xe = [_mm(xi, ei) for xi, ei in zip(x, e)]
        x = [xi + _mm(xei, xi) for xi, xei in zip(x, xe)]
        k *= 2
    return x


def _rwkv_body(r_ref, k_ref, v_ref, l_ref, p_ref, mul_ref, du_ref, iu_ref, gu_ref, bones_ref, lt_ref,
               o_ref, z_ref, carry_ref):
    c = RWKV_CHUNK
    aw = r_ref.shape[1]
    lw = l_ref.shape[1]
    n_pairs = aw // V7X_LANES

    @pl.when(pl.program_id(1) == 0)
    def _():
        z_ref[...] = jnp.zeros_like(z_ref)
        carry_ref[...] = jnp.zeros_like(carry_ref)

    row1 = lax.broadcasted_iota(jnp.int32, (c, 1), 0)

    def token_shift(raw, prev_last, mu):
        prev = jnp.where(row1 == 0, prev_last, pltpu.roll(raw, 1, 0))
        return raw + mu * (prev - raw)

    prm = p_ref[...]
    mu_r, mu_k, mu_v, w0, a0, k_k, k_a, r_k, gn_w, gn_b = (prm[i:i + 1] for i in range(10))

    r_raw = r_ref[...].astype(F32)
    k_raw = k_ref[...].astype(F32)
    v_raw = v_ref[...].astype(F32)
    l_raw = l_ref[...].astype(F32)
    r = token_shift(r_raw, carry_ref[0:1, 0:aw], mu_r)
    k = token_shift(k_raw, carry_ref[0:1, aw:2 * aw], mu_k)
    v = token_shift(v_raw, carry_ref[0:1, 2 * aw:3 * aw], mu_v)
    xl = token_shift(l_raw, carry_ref[0:1, 3 * aw:3 * aw + lw], mul_ref[0:1])
    carry_ref[0:1, 0:aw] = r_raw[c - 1:c]
    carry_ref[0:1, aw:2 * aw] = k_raw[c - 1:c]
    carry_ref[0:1, 2 * aw:3 * aw] = v_raw[c - 1:c]
    carry_ref[0:1, 3 * aw:3 * aw + lw] = l_raw[c - 1:c]

    l0 = xl[:, :V7X_LANES]
    l1 = xl[:, V7X_LANES:]
    dw = _mm(jnp.tanh(l0), du_ref[...])
    da = _mm(l0, iu_ref[...])
    g = _mm(_sigmoid(l1), gu_ref[...])
    w_log = -_softplus(-(w0 + dw)) - 0.5
    logw = -jnp.exp(w_log)
    a = _sigmoid(a0 + da)

    bones = bones_ref[...]
    kk = k * k_k
    ssq = _mm_hilo(kk * kk, bones)
    kk = kk / jnp.maximum(jnp.sqrt(ssq), 1e-12)
    k_mod = k * (1.0 + (a - 1.0) * k_a)
    a_vec = -kk
    b_vec = kk * a

    cum = _hilo_mm(lt_ref[...], logw)
    cmid = cum[c // 2 - 1:c // 2]
    cend = cum[c - 1:c]
    emid = jnp.exp(cmid)
    a_s = a_vec * jnp.exp(cum - logw - cmid)
    a_t = a_s * emid
    r_s = r * jnp.exp(cum - cmid)
    r_t = r_s * emid
    e_neg = jnp.exp(cmid - cum)
    b_s = b_vec * e_neg
    k_s = k_mod * e_neg
    e_end = jnp.exp(cend - cum)
    b_p = b_vec * e_end
    k_p = k_mod * e_end
    p_end = jnp.exp(cend)

    row = lax.broadcasted_iota(jnp.int32, (c, c), 0)
    col = lax.broadcasted_iota(jnp.int32, (c, c), 1)
    strict = col < row
    incl = col <= row
    eye = row == col
    same_head = (row // HEAD_DIM) == (col // HEAD_DIM)
    m0 = lax.broadcasted_iota(jnp.int32, (1, V7X_LANES), 1) < HEAD_DIM

    def stack(x):
        return jnp.concatenate([jnp.where(m0, x, 0.0), jnp.where(m0, 0.0, x)], axis=0)

    pairs = range(n_pairs)
    sls = [slice(p * V7X_LANES, (p + 1) * V7X_LANES) for p in pairs]
    v_p = [v[:, sl] for sl in sls]
    gm = [_mm_nt(jnp.concatenate([stack(a_s[:, sl]), stack(r_s[:, sl])], axis=0),
                 jnp.concatenate([b_s[:, sl], k_s[:, sl]], axis=0)) for sl in sls]
    aab = [jnp.where(strict, gm[p][h * c:(h + 1) * c, 0:c], 0.0) for p in pairs for h in range(2)]
    tinv = _tri_inverse_many(aab, row, col)
    aak = [jnp.concatenate([jnp.where(strict, gm[p][h * c:(h + 1) * c, c:2 * c], 0.0) for h in range(2)],
                           axis=1) for p in pairs]
    arb = [jnp.concatenate([jnp.where(incl, gm[p][(2 + h) * c:(3 + h) * c, 0:c], 0.0) for h in range(2)],
                           axis=1) for p in pairs]
    ark = [jnp.concatenate([jnp.where(incl, gm[p][(2 + h) * c:(3 + h) * c, c:2 * c], 0.0) for h in range(2)],
                           axis=1) for p in pairs]
    vs = [stack(x) for x in v_p]
    akv = [_mm(aak[p], vs[p]) for p in pairs]
    wu = [_mm(jnp.concatenate([tinv[2 * p], tinv[2 * p + 1]], axis=1),
              jnp.concatenate([stack(a_t[:, sls[p]]), stack(akv[p])], axis=1)) for p in pairs]
    w_t = [x[:, :V7X_LANES] for x in wu]
    u_v = [x[:, V7X_LANES:] for x in wu]
    q_t = [r_t[:, sls[p]] + _mm(arb[p], stack(w_t[p])) for p in pairs]
    y_v = [_mm(jnp.concatenate([arb[p], ark[p]], axis=1),
               jnp.concatenate([stack(u_v[p]), vs[p]], axis=0)) for p in pairs]
    bp_t = [b_p[:, sl].T for sl in sls]
    kp_t = [k_p[:, sl].T for sl in sls]
    m_mat = [jnp.where(same_head, _mm(bp_t[p], w_t[p]), 0.0) for p in pairs]
    n_mat = [jnp.where(same_head, _mm(jnp.concatenate([bp_t[p], kp_t[p]], axis=1),
                                      jnp.concatenate([u_v[p], v_p[p]], axis=0)), 0.0) for p in pairs]
    z_old = [z_ref[p] for p in pairs]
    y_parts = [_mm(q_t[p], z_old[p]) + y_v[p] for p in pairs]
    for p in pairs:
        p_col = jnp.sum(jnp.where(eye, p_end[:, sls[p]], 0.0), axis=1, keepdims=True)
        z_ref[p] = p_col * z_old[p] + _mm(m_mat[p], z_old[p]) + n_mat[p]

    y = jnp.concatenate(y_parts, axis=1)
    inv_hd = 1.0 / HEAD_DIM
    mean = _mm_hilo(y, bones) * inv_hd
    d = y - mean
    var = _mm_hilo(d * d, bones) * inv_hd
    yn = d * lax.rsqrt(var + GN_EPS) * gn_w + gn_b
    bonus = _mm_hilo(r * k_mod * r_k, bones) * v
    o_ref[...] = ((yn + bonus) * g).astype(o_ref.dtype)


def _rwkv(r, k, v, l, prm, mul, du, iu, gu, bsz, t_len):
    n, aw = r.shape
    lw = l.shape[1]
    c = RWKV_CHUNK
    nt = t_len // c
    heads = np.arange(aw) // HEAD_DIM
    bones = jnp.asarray((heads[:, None] == heads[None, :]).astype(np.float32), BF16)
    lt = jnp.asarray(np.tril(np.ones((c, c), np.float32)), BF16)
    tok = lambda w: pl.BlockSpec((c, w), lambda b, t: (b * nt + t, 0))
    full = lambda arr: pl.BlockSpec(arr.shape, lambda b, t: (0,) * arr.ndim)
    return pl.pallas_call(
        _rwkv_body,
        grid=(bsz, nt),
        in_specs=[tok(aw), tok(aw), tok(aw), tok(lw), full(prm), full(mul), full(du), full(iu), full(gu),
                  full(bones), full(lt)],
        out_specs=tok(aw),
        out_shape=jax.ShapeDtypeStruct((n, aw), BF16),
        scratch_shapes=[pltpu.VMEM((aw // V7X_LANES, V7X_LANES, V7X_LANES), F32),
                        pltpu.VMEM((V7X_SUBLANES, 3 * aw + lw), F32)],
        compiler_params=_cparams(("parallel", "arbitrary")),
    )(r, k, v, l, prm, mul, du, iu, gu, bones, lt)


def _sb_body(q_ref, k_ref, v_ref, qg_ref, kg_ref, bo_ref, su_ref, o_ref, qm_ref, kn_ref, acc_ref, car_ref, *,
             kgroup):
    t_len = q_ref.shape[0]
    blk = SB_BLOCK
    m0 = lax.broadcasted_iota(jnp.int32, (1, V7X_LANES), 1) < HEAD_DIM
    scale = 1.0 / math.sqrt(HEAD_DIM)
    inv_hd = 1.0 / HEAD_DIM

    def prep(i, _):
        s0 = pl.multiple_of(i * blk, blk)
        q = q_ref[pl.ds(s0, blk), :].astype(F32)
        qn = q * lax.rsqrt(_mm_hilo(q * q, bo_ref[...]) * inv_hd + RMS_EPS) * (qg_ref[...] * scale)
        qm_ref[0, pl.ds(s0, blk), :] = jnp.where(m0, qn, 0.0).astype(BF16)
        qm_ref[1, pl.ds(s0, blk), :] = jnp.where(m0, 0.0, qn).astype(BF16)
        kx = k_ref[pl.ds(s0, blk), :].astype(F32)
        kn = kx * lax.rsqrt(_mm_hilo(kx * kx, bo_ref[...]) * inv_hd + RMS_EPS) * kg_ref[...]
        kn_ref[pl.ds(s0, blk), :] = kn.astype(BF16)
        return 0

    lax.fori_loop(0, t_len // blk, prep, 0)

    wide = kgroup * blk
    row = lax.broadcasted_iota(jnp.int32, (blk, wide), 0)
    col = lax.broadcasted_iota(jnp.int32, (blk, wide), 1)
    heads = range(2)

    def process(q0, k0, diag):
        kb = kn_ref[pl.ds(k0, wide), :]
        vb = v_ref[pl.ds(k0, wide), :]
        z = [_mm_nt(qm_ref[h, pl.ds(q0, blk), :], kb) for h in heads]
        lg = [-_softplus(zh) for zh in z]
        if diag:
            causal = (col + k0) < (row + q0)
            lg = [jnp.where(causal, x, 0.0) for x in lg]
        hilo = [_split_hilo(x) for x in lg]
        su = su_ref[...]
        rs = [[jnp.dot(hilo[h][0][:, b * blk:(b + 1) * blk], su, preferred_element_type=F32)
               + jnp.dot(hilo[h][1][:, b * blk:(b + 1) * blk], su, preferred_element_type=F32)
               for b in range(kgroup)] for h in heads]
        att = []
        for h in heads:
            car = car_ref[h]
            after = [None] * kgroup
            for b in reversed(range(kgroup)):
                after[b] = rs[h][b][:, :blk] + car
                car = car + rs[h][b][:, blk:]
            car_ref[h] = car
            logit = z[h] + lg[h] + jnp.concatenate(after, axis=1)
            if diag:
                logit = jnp.where(causal, logit, NEG_BIG)
            att.append(jnp.exp(logit))
        for h in heads:
            acc_ref[h] += _mm(att[h], vb)

    def qblock(i, _):
        q0 = pl.multiple_of(i * blk, blk)
        acc_ref[...] = jnp.zeros_like(acc_ref)
        car_ref[...] = jnp.zeros_like(car_ref)
        sd = i // kgroup
        process(q0, pl.multiple_of(sd * wide, wide), True)

        def kstep(jj, _):
            process(q0, pl.multiple_of((sd - 1 - jj) * wide, wide), False)
            return 0

        lax.fori_loop(0, sd, kstep, 0)
        o_ref[pl.ds(q0, blk), :] = jnp.where(m0, acc_ref[0], acc_ref[1]).astype(o_ref.dtype)
        return 0

    lax.fori_loop(0, t_len // blk, qblock, 0)


def _stick_breaking(q, k, v, qg, kg, bsz, t_len):
    n, bw = q.shape
    n_pairs = bw // V7X_LANES
    blk = SB_BLOCK
    heads = np.arange(V7X_LANES) // HEAD_DIM
    bo = jnp.asarray((heads[:, None] == heads[None, :]).astype(np.float32), BF16)
    idx = np.arange(blk)
    su = np.concatenate([(idx[:, None] > idx[None, :]).astype(np.float32), np.ones((blk, blk), np.float32)], axis=1)
    su = jnp.asarray(su, BF16)
    tok = pl.BlockSpec((t_len, V7X_LANES), lambda b, p: (b, p))
    full = lambda arr: pl.BlockSpec(arr.shape, lambda b, p: (0,) * arr.ndim)
    return pl.pallas_call(
        functools.partial(_sb_body, kgroup=math.gcd(SB_KEY_GROUP, t_len // blk)),
        grid=(bsz, n_pairs),
        in_specs=[tok, tok, tok, full(qg), full(kg), full(bo), full(su)],
        out_specs=tok,
        out_shape=jax.ShapeDtypeStruct((n, bw), BF16),
        scratch_shapes=[pltpu.VMEM((2, t_len, V7X_LANES), BF16),
                        pltpu.VMEM((t_len, V7X_LANES), BF16),
                        pltpu.VMEM((2, blk, V7X_LANES), F32),
                        pltpu.VMEM((2, blk, blk), F32)],
        compiler_params=_cparams(("parallel", "parallel")),
    )(q, k, v, qg, kg, bo, su)


def _lru_body(gate_ref, rec_ref, cw_ref, pv_ref, wr_ref, wi_ref, o_ref, prev_ref, h_ref):
    tt = rec_ref.shape[0]
    width = rec_ref.shape[1]
    n_blocks = width // V7X_LANES

    @pl.when(pl.program_id(1) == 0)
    def _():
        prev_ref[...] = jnp.zeros_like(prev_ref)
        h_ref[...] = jnp.zeros_like(h_ref)

    row = lax.broadcasted_iota(jnp.int32, (tt, 1), 0)
    row8 = lax.broadcasted_iota(jnp.int32, (V7X_SUBLANES, 1), 0)

    for gi in range(n_blocks):
        sl = slice(gi * V7X_LANES, (gi + 1) * V7X_LANES)
        u = rec_ref[:, sl].astype(F32)
        prev = prev_ref[:, sl]

        def shifted(s):
            ru = pltpu.roll(u, s, 0)
            first = jnp.where(row8 < s, pltpu.roll(prev, s, 0), ru[0:V7X_SUBLANES])
            return jnp.concatenate([first, ru[V7X_SUBLANES:]], axis=0)

        cw = cw_ref[:, sl]
        rc = (cw[3:4] * u + cw[2:3] * shifted(1) + cw[1:2] * shifted(2) + cw[0:1] * shifted(3)
              + pv_ref[0:1, sl])
        prev_ref[:, sl] = u[tt - V7X_SUBLANES:tt]

        rgate = _sigmoid(_mm(rc, wr_ref[gi]) + pv_ref[1:2, sl])
        igate = _sigmoid(_mm(rc, wi_ref[gi]) + pv_ref[2:3, sl])
        log_a = -LRU_C * rgate * _softplus(-pv_ref[3:4, sl])
        a = jnp.exp(log_a)
        b = jnp.sqrt(1.0 - jnp.exp(2.0 * log_a)) * (igate * rc)

        s = 1
        while s < tt:
            keep = row >= s
            a_sh = jnp.where(keep, pltpu.roll(a, s, 0), 1.0)
            b_sh = jnp.where(keep, pltpu.roll(b, s, 0), 0.0)
            b = a * b_sh + b
            a = a * a_sh
            s *= 2
        h = a * h_ref[0:1, sl] + b
        h_ref[0:1, sl] = h[tt - 1:tt]
        o_ref[:, sl] = (jax.nn.gelu(gate_ref[:, sl].astype(F32)) * h).astype(o_ref.dtype)


def _rg_lru(gate, rec, cw, pv, wr, wi, bsz, t_len, tt):
    n, width = rec.shape
    nt = t_len // tt
    tok = pl.BlockSpec((tt, width), lambda b, t: (b * nt + t, 0))
    full = lambda arr: pl.BlockSpec(arr.shape, lambda b, t: (0,) * arr.ndim)
    return pl.pallas_call(
        _lru_body,
        grid=(bsz, nt),
        in_specs=[tok, tok, full(cw), full(pv), full(wr), full(wi)],
        out_specs=tok,
        out_shape=jax.ShapeDtypeStruct((n, width), BF16),
        scratch_shapes=[pltpu.VMEM((V7X_SUBLANES, width), F32), pltpu.VMEM((V7X_SUBLANES, width), F32)],
        compiler_params=_cparams(("parallel", "arbitrary")),
    )(gate, rec, cw, pv, wr, wi)


def _norm_route_body(x_ref, g_ref, wh_ref, wl_ref, b_ref, slt_ref, h_ref, rt_ref, cnt_ref, run_ref):
    @pl.when(pl.program_id(0) == 0)
    def _():
        run_ref[...] = jnp.zeros_like(run_ref)

    x = x_ref[...]
    ms = jnp.mean(x * x, axis=-1, keepdims=True)
    h = x * lax.rsqrt(ms + RMS_EPS) * g_ref[...]
    h_hi, h_lo = _split_hilo(h)
    h_ref[...] = h_hi
    wh = wh_ref[...]
    lg = (jnp.dot(h_hi, wh, preferred_element_type=F32) + jnp.dot(h_lo, wh, preferred_element_type=F32)
          + jnp.dot(h_hi, wl_ref[...], preferred_element_type=F32)) + b_ref[0:1]
    lane = lax.broadcasted_iota(jnp.int32, lg.shape, 1)
    neg_inf = -jnp.inf
    is_g = lane < N_GROUPS
    gl = jnp.where(is_g, lg, neg_inf)
    gmax = jnp.max(gl, axis=-1, keepdims=True)
    gidx = jnp.min(jnp.where(gl == gmax, lane, V7X_LANES), axis=-1, keepdims=True)
    g_top = 1.0 / jnp.sum(jnp.where(is_g, jnp.exp(gl - gmax), 0.0), axis=-1, keepdims=True)
    lo_e = N_GROUPS + EXPERTS_PER_GROUP * gidx
    el = jnp.where((lane >= lo_e) & (lane < lo_e + EXPERTS_PER_GROUP), lg, neg_inf)
    m1 = jnp.max(el, axis=-1, keepdims=True)
    i1 = jnp.min(jnp.where(el == m1, lane, V7X_LANES), axis=-1, keepdims=True)
    el2 = jnp.where(lane == i1, neg_inf, el)
    m2 = jnp.max(el2, axis=-1, keepdims=True)
    i2 = jnp.min(jnp.where(el2 == m2, lane, V7X_LANES), axis=-1, keepdims=True)
    t = jnp.exp(m2 - m1)
    gate1 = g_top / (1.0 + t)
    gate2 = g_top * t / (1.0 + t)
    oh1 = lane == i1
    oh2 = lane == i2
    oh = jnp.where(oh1 | oh2, 1.0, 0.0)
    before = jnp.dot(slt_ref[...], oh.astype(BF16), preferred_element_type=F32) + run_ref[0:1]
    rank1 = jnp.sum(jnp.where(oh1, before, 0.0), axis=-1, keepdims=True)
    rank2 = jnp.sum(jnp.where(oh2, before, 0.0), axis=-1, keepdims=True)
    run_ref[0:1] = run_ref[0:1] + jnp.sum(oh, axis=0, keepdims=True)
    cnt_ref[...] = jnp.broadcast_to(run_ref[0:1], cnt_ref.shape)
    vals = (i1 - N_GROUPS, i2 - N_GROUPS, rank1, rank2, gate1, gate2)
    out = jnp.zeros(lg.shape, F32)
    for li, val in enumerate(vals):
        out = jnp.where(lane == li, val.astype(F32), out)
    rt_ref[...] = out


def _norm_route(x, g, w_router, b_router, tm):
    n, d = x.shape
    nr = w_router.shape[1]
    wpad = jnp.zeros((d, V7X_LANES), F32).at[:, :nr].set(w_router)
    wh = wpad.astype(BF16)
    wl = (wpad - wh.astype(F32)).astype(BF16)
    bias = jnp.zeros((V7X_SUBLANES, V7X_LANES), F32).at[0, :nr].set(b_router)
    slt = jnp.asarray(np.tril(np.ones((tm, tm), np.float32), -1), BF16)
    const = lambda shape: pl.BlockSpec(shape, lambda i: (0, 0))
    return pl.pallas_call(
        _norm_route_body,
        grid=(n // tm,),
        in_specs=[pl.BlockSpec((tm, d), lambda i: (i, 0)), const((1, d)), const((d, V7X_LANES)),
                  const((d, V7X_LANES)), const((V7X_SUBLANES, V7X_LANES)), const((tm, tm))],
        out_specs=[pl.BlockSpec((tm, d), lambda i: (i, 0)),
                   pl.BlockSpec((tm, V7X_LANES), lambda i: (i, 0)),
                   const((V7X_SUBLANES, V7X_LANES))],
        out_shape=[jax.ShapeDtypeStruct((n, d), BF16), jax.ShapeDtypeStruct((n, V7X_LANES), F32),
                   jax.ShapeDtypeStruct((V7X_SUBLANES, V7X_LANES), F32)],
        scratch_shapes=[pltpu.VMEM((V7X_SUBLANES, V7X_LANES), F32)],
        compiler_params=_cparams(("arbitrary",)),
    )(x, g, wh, wl, bias, slt)


def _moe_body(te_ref, nv_ref, x_ref, wg_ref, wu_ref, wd_ref, o_ref):
    @pl.when(pl.program_id(0) < nv_ref[0])
    def _():
        x = x_ref[...]
        gt = jnp.dot(x, wg_ref[0], preferred_element_type=F32)
        up = jnp.dot(x, wu_ref[0], preferred_element_type=F32)
        hid = (gt * _sigmoid(gt)) * up
        o_ref[...] = jnp.dot(hid.astype(BF16), wd_ref[0], preferred_element_type=F32).astype(o_ref.dtype)


def _moe_grouped(xs, tile_expert, n_valid, wg, wu, wd):
    p_rows, d = xs.shape
    de = wg.shape[2]
    tm = MOE_TILE
    grid_spec = pltpu.PrefetchScalarGridSpec(
        num_scalar_prefetch=2,
        grid=(p_rows // tm,),
        in_specs=[pl.BlockSpec((tm, d), lambda i, te, nv: (i, 0)),
                  pl.BlockSpec((1, d, de), lambda i, te, nv: (te[i], 0, 0)),
                  pl.BlockSpec((1, d, de), lambda i, te, nv: (te[i], 0, 0)),
                  pl.BlockSpec((1, de, d), lambda i, te, nv: (te[i], 0, 0))],
        out_specs=pl.BlockSpec((tm, d), lambda i, te, nv: (i, 0)),
    )
    return pl.pallas_call(
        _moe_body,
        grid_spec=grid_spec,
        out_shape=jax.ShapeDtypeStruct((p_rows, d), BF16),
        compiler_params=_cparams(("arbitrary",)),
    )(tile_expert, n_valid, xs, wg, wu, wd)


def _hier_moe(x, g_norm, w_group, b_group, w_erouter, b_erouter, wg, wu, wd, tm):
    n, d = x.shape
    hn, route, cnt = _norm_route(x, g_norm, jnp.concatenate([w_group, w_erouter], axis=1),
                                 jnp.concatenate([b_group, b_erouter]), tm)
    expert_id = route[:, 0:2].astype(jnp.int32)
    rank = route[:, 2:4].astype(jnp.int32)
    gates = route[:, 4:6]
    counts = cnt[0, N_GROUPS:N_GROUPS + N_EXPERTS].astype(jnp.int32)

    tmm = MOE_TILE
    padded = ((counts + tmm - 1) // tmm) * tmm
    ends = jnp.cumsum(padded)
    starts = ends - padded
    pos = starts[expert_id] + rank
    p_rows = 2 * n + N_EXPERTS * tmm
    n_tiles = p_rows // tmm
    tok = jnp.broadcast_to(jnp.arange(n, dtype=jnp.int32)[:, None], (n, 2))
    row_token = jnp.zeros((p_rows,), jnp.int32).at[pos.reshape(-1)].set(tok.reshape(-1))
    tile_expert = jnp.minimum(
        jnp.searchsorted(ends, jnp.arange(n_tiles, dtype=jnp.int32) * tmm, side="right"), N_EXPERTS - 1
    ).astype(jnp.int32)
    n_valid = (ends[-1] // tmm).astype(jnp.int32).reshape(1)

    xs = jnp.take(hn, row_token, axis=0)
    ys = _moe_grouped(xs, tile_expert, n_valid, wg, wu, wd)
    y = (gates[:, 0:1] * jnp.take(ys, pos[:, 0], axis=0).astype(F32)
         + gates[:, 1:2] * jnp.take(ys, pos[:, 1], axis=0).astype(F32))
    return x + y


def _pad_rows(w, rows, offset=0):
    out = jnp.zeros((rows, w.shape[1]), w.dtype)
    return out.at[offset:offset + w.shape[0]].set(w)


def kernel(x, norm_mix, norm_ffn, w_in_even, mu_a, w0, decay_up, a0, iclr_up, gate_up, k_k, k_a, r_k, gn_w,
           gn_b, q_norm_g, k_norm_g, w_out_even, w_in_odd, conv_w, conv_b, w_rgate, b_rgate, w_igate,
           b_igate, lru_lambda, w_out_odd, w_group, b_group, w_erouter, b_erouter, exp_w_gate, exp_w_up,
           exp_w_down):
    bsz, t_len, d = x.shape
    n = bsz * t_len
    depth = norm_mix.shape[0]
    aw = decay_up.shape[2]
    d_lora, i_lora, g_lora = decay_up.shape[1], iclr_up.shape[1], gate_up.shape[1]
    bw = (w_in_even.shape[2] - (3 * aw + d_lora + i_lora + g_lora)) // 3
    lw = w_rgate.shape[1] * w_rgate.shape[2]
    assert d_lora + i_lora == V7X_LANES and g_lora <= 2 * V7X_LANES
    lora_w = 3 * V7X_LANES
    tm = 512
    xf = x.reshape(n, d)

    for layer in range(depth):
        gm = norm_mix[layer].reshape(1, d)
        if layer % 2 == 0:
            i = layer // 2
            w_in = w_in_even[i]
            o3 = 3 * aw
            a_cols = o3 + d_lora + i_lora + g_lora
            pad = jnp.zeros((d, lora_w - (d_lora + i_lora + g_lora)), F32)
            w_cat = jnp.concatenate([w_in[:, :a_cols], pad, w_in[:, a_cols:]], axis=1).astype(BF16)
            r, k, v, lo, qb, kb, vb = _norm_matmul(xf, gm, w_cat, (aw, aw, aw, lora_w, bw, bw, bw), tm)
            mu = mu_a[i]
            rows = [mu[:aw], mu[aw:2 * aw], mu[2 * aw:o3], w0[i], a0[i], k_k[i], k_a[i], r_k[i].reshape(-1),
                    gn_w[i], gn_b[i]]
            prm = jnp.zeros((16, aw), F32).at[:len(rows)].set(jnp.stack(rows))
            mul = jnp.zeros((V7X_SUBLANES, lora_w), F32).at[0, :a_cols - o3].set(mu[o3:])
            du = _pad_rows(decay_up[i], V7X_LANES, 0).astype(BF16)
            iu = _pad_rows(iclr_up[i], V7X_LANES, d_lora).astype(BF16)
            gu = _pad_rows(gate_up[i], 2 * V7X_LANES, 0).astype(BF16)
            ya = _rwkv(r, k, v, lo, prm, mul, du, iu, gu, bsz, t_len)
            qg = jnp.tile(q_norm_g[i], V7X_LANES // HEAD_DIM).reshape(1, V7X_LANES)
            kg = jnp.tile(k_norm_g[i], V7X_LANES // HEAD_DIM).reshape(1, V7X_LANES)
            yb = _stick_breaking(qb, kb, vb, qg, kg, bsz, t_len)
            w_out = w_out_even[i].astype(BF16)
            xf = _proj_residual(xf, [ya, yb], [w_out[:aw], w_out[aw:]], tm)
        else:
            j = layer // 2
            gate, rec = _norm_matmul(xf, gm, w_in_odd[j].astype(BF16), (lw, lw), tm)
            cw = jnp.zeros((V7X_SUBLANES, lw), F32).at[:conv_w.shape[1]].set(conv_w[j])
            pv = jnp.zeros((V7X_SUBLANES, lw), F32).at[:4].set(
                jnp.stack([conv_b[j], b_rgate[j], b_igate[j], lru_lambda[j]]))
            yl = _rg_lru(gate, rec, cw, pv, w_rgate[j].astype(BF16), w_igate[j].astype(BF16), bsz, t_len,
                         min(256, t_len))
            xf = _proj_residual(xf, [yl], [w_out_odd[j].astype(BF16)], tm)
        xf = _hier_moe(xf, norm_ffn[layer].reshape(1, d), w_group[layer], b_group[layer], w_erouter[layer],
                       b_erouter[layer], exp_w_gate[layer].astype(BF16), exp_w_up[layer].astype(BF16),
                       exp_w_down[layer].astype(BF16), tm)
    return xf.reshape(bsz, t_len, d)
```

```python
import functools
import math

import numpy as np
import jax
import jax.numpy as jnp
from jax import lax
from jax.experimental import pallas as pl
from jax.experimental.pallas import tpu as pltpu

F32 = jnp.float32
BF16 = jnp.bfloat16

V7X_LANES = 128
V7X_SUBLANES = 8
V7X_VMEM_LIMIT_BYTES = 48 * 1024 * 1024

HEAD_DIM = 64
RMS_EPS = 1e-6
GN_EPS = 64e-5
LRU_C = 8.0
N_GROUPS = 4
EXPERTS_PER_GROUP = 4
N_EXPERTS = 16
RWKV_CHUNK = 128
SB_BLOCK = 128
SB_KEY_GROUP = 4
SB_QUERY_GROUP = 2
MOE_TILE = 512
NEG_BIG = -1e30
LOG2_E = 1.4426950408889634


def _cparams(sem):
    return pltpu.CompilerParams(dimension_semantics=sem, vmem_limit_bytes=V7X_VMEM_LIMIT_BYTES)


def _mm(a, b):
    return jnp.dot(a.astype(BF16), b.astype(BF16), preferred_element_type=F32)


def _mm_nt(a, b):
    return lax.dot_general(a.astype(BF16), b.astype(BF16), (((1,), (1,)), ((), ())),
                           preferred_element_type=F32)


def _split_hilo(x):
    hi = x.astype(BF16)
    lo = (x - hi.astype(F32)).astype(BF16)
    return hi, lo


def _mm_hilo(x, m):
    hi, lo = _split_hilo(x)
    return (jnp.dot(hi, m, preferred_element_type=F32) + jnp.dot(lo, m, preferred_element_type=F32))


def _hilo_mm(m, x):
    hi, lo = _split_hilo(x)
    return (jnp.dot(m, hi, preferred_element_type=F32) + jnp.dot(m, lo, preferred_element_type=F32))


def _sigmoid(x):
    return 0.5 * jnp.tanh(0.5 * x) + 0.5


def _neg_abs(x):
    bits = pltpu.bitcast(x, jnp.uint32) | jnp.uint32(0x80000000)
    return pltpu.bitcast(bits, F32)


def _softplus(x):
    return jnp.maximum(x, 0.0) + jnp.log(1.0 + jnp.exp(-jnp.abs(x)))


def _norm_matmul_body(x_ref, g_ref, w_ref, *out_refs, col_splits):
    x = x_ref[...]
    ms = jnp.mean(x * x, axis=-1, keepdims=True)
    h = (x * lax.rsqrt(ms + RMS_EPS) * g_ref[...]).astype(BF16)
    off = 0
    for o_ref, c in zip(out_refs, col_splits):
        o_ref[...] = jnp.dot(h, w_ref[:, off:off + c], preferred_element_type=F32).astype(o_ref.dtype)
        off += c


def _norm_matmul(x, g, w, col_splits, tm):
    n, d = x.shape
    ctot = w.shape[1]
    assert sum(col_splits) == ctot and n % tm == 0
    return pl.pallas_call(
        functools.partial(_norm_matmul_body, col_splits=tuple(col_splits)),
        grid=(n // tm,),
        in_specs=[pl.BlockSpec((tm, d), lambda i: (i, 0)),
                  pl.BlockSpec((1, d), lambda i: (0, 0)),
                  pl.BlockSpec((d, ctot), lambda i: (0, 0))],
        out_specs=[pl.BlockSpec((tm, c), lambda i: (i, 0)) for c in col_splits],
        out_shape=[jax.ShapeDtypeStruct((n, c), BF16) for c in col_splits],
        compiler_params=_cparams(("parallel",)),
    )(x, g, w)


def _tri_inverse_many(mats, row, col):
    eye = (row == col).astype(F32)
    blk8 = (row // 8) == (col // 8)
    n1 = [jnp.where(blk8, a, 0.0) for a in mats]
    n2 = [_mm(n, n) for n in n1]
    ipn = [eye + n for n in n1]
    p = [i + _mm(i, m) for i, m in zip(ipn, n2)]
    n4 = [_mm(m, m) for m in n2]
    x = [q + _mm(q, m) for q, m in zip(p, n4)]
    k = 8
    c = mats[0].shape[0]
    while k < c:
        sel = ((row // (2 * k)) == (col // (2 * k))) & ((row // k) > (col // k))
        e = [jnp.where(sel, a, 0.0) for a in mats]
        xe = [_mm(xi, ei) for xi, ei in zip(x, e)]
        x = [xi + _mm(xei, xi) for xi, xei in zip(x, xe)]
        k *= 2
    return x


def _rwkv_body(r_ref, k_ref, v_ref, l_ref, p_ref, mul_ref, du_ref, iu_ref, gu_ref, bones_ref, lt_ref,
               o_ref, z_ref, carry_ref):
    c = RWKV_CHUNK
    aw = r_ref.shape[1]
    lw = l_ref.shape[1]
    n_pairs = aw // V7X_LANES

    @pl.when(pl.program_id(1) == 0)
    def _():
        z_ref[...] = jnp.zeros_like(z_ref)
        carry_ref[...] = jnp.zeros_like(carry_ref)

    row1 = lax.broadcasted_iota(jnp.int32, (c, 1), 0)

    def token_shift(raw, prev_last, mu):
        prev = jnp.where(row1 == 0, prev_last, pltpu.roll(raw, 1, 0))
        return raw + mu * (prev - raw)

    prm = p_ref[...]
    mu_r, mu_k, mu_v, w0, a0, k_k, k_a, r_k, gn_w, gn_b = (prm[i:i + 1] for i in range(10))

    r_raw = r_ref[...].astype(F32)
    k_raw = k_ref[...].astype(F32)
    v_raw = v_ref[...].astype(F32)
    l_raw = l_ref[...].astype(F32)
    r = token_shift(r_raw, carry_ref[0:1, 0:aw], mu_r)
    k = token_shift(k_raw, carry_ref[0:1, aw:2 * aw], mu_k)
    v = token_shift(v_raw, carry_ref[0:1, 2 * aw:3 * aw], mu_v)
    xl = token_shift(l_raw, carry_ref[0:1, 3 * aw:3 * aw + lw], mul_ref[0:1])
    carry_ref[0:1, 0:aw] = r_raw[c - 1:c]
    carry_ref[0:1, aw:2 * aw] = k_raw[c - 1:c]
    carry_ref[0:1, 2 * aw:3 * aw] = v_raw[c - 1:c]
    carry_ref[0:1, 3 * aw:3 * aw + lw] = l_raw[c - 1:c]

    l0 = xl[:, :V7X_LANES]
    l1 = xl[:, V7X_LANES:]
    dw = _mm(jnp.tanh(l0), du_ref[...])
    da = _mm(l0, iu_ref[...])
    g = _mm(_sigmoid(l1), gu_ref[...])
    w_log = -_softplus(-(w0 + dw)) - 0.5
    logw = -jnp.exp(w_log)
    a = _sigmoid(a0 + da)

    bones = bones_ref[...]
    kk = k * k_k
    ssq = _mm_hilo(kk * kk, bones)
    kk = kk * jnp.minimum(lax.rsqrt(ssq), 1e12)
    k_mod = k * (1.0 + (a - 1.0) * k_a)
    a_vec = -kk
    b_vec = kk * a

    cum = _hilo_mm(lt_ref[...], logw)
    cmid = cum[c // 2 - 1:c // 2]
    cend = cum[c - 1:c]
    emid = jnp.exp(cmid)
    a_s = a_vec * jnp.exp(cum - logw - cmid)
    a_t = a_s * emid
    r_s = r * jnp.exp(cum - cmid)
    r_t = r_s * emid
    e_neg = jnp.exp(cmid - cum)
    b_s = b_vec * e_neg
    k_s = k_mod * e_neg
    e_end = jnp.exp(cend - cum)
    b_p = b_vec * e_end
    k_p = k_mod * e_end
    p_end = jnp.exp(cend)

    row = lax.broadcasted_iota(jnp.int32, (c, c), 0)
    col = lax.broadcasted_iota(jnp.int32, (c, c), 1)
    strict = col < row
    incl = col <= row
    eye = row == col
    same_head = (row // HEAD_DIM) == (col // HEAD_DIM)
    m0 = lax.broadcasted_iota(jnp.int32, (1, V7X_LANES), 1) < HEAD_DIM

    def stack(x):
        return jnp.concatenate([jnp.where(m0, x, 0.0), jnp.where(m0, 0.0, x)], axis=0)

    pairs = range(n_pairs)
    sls = [slice(p * V7X_LANES, (p + 1) * V7X_LANES) for p in pairs]
    v_p = [v[:, sl] for sl in sls]
    gm = [_mm_nt(jnp.concatenate([stack(a_s[:, sl]), stack(r_s[:, sl])], axis=0),
                 jnp.concatenate([b_s[:, sl], k_s[:, sl]], axis=0)) for sl in sls]
    aab = [jnp.where(strict, gm[p][h * c:(h + 1) * c, 0:c], 0.0) for p in pairs for h in range(2)]
    tinv = _tri_inverse_many(aab, row, col)
    aak = [jnp.concatenate([jnp.where(strict, gm[p][h * c:(h + 1) * c, c:2 * c], 0.0) for h in range(2)],
                           axis=1) for p in pairs]
    arb = [jnp.concatenate([jnp.where(incl, gm[p][(2 + h) * c:(3 + h) * c, 0:c], 0.0) for h in range(2)],
                           axis=1) for p in pairs]
    ark = [jnp.concatenate([jnp.where(incl, gm[p][(2 + h) * c:(3 + h) * c, c:2 * c], 0.0) for h in range(2)],
                           axis=1) for p in pairs]
    vs = [stack(x) for x in v_p]
    akv = [_mm(aak[p], vs[p]) for p in pairs]
    wu = [_mm(jnp.concatenate([tinv[2 * p], tinv[2 * p + 1]], axis=1),
              jnp.concatenate([stack(a_t[:, sls[p]]), stack(akv[p])], axis=1)) for p in pairs]
    w_t = [x[:, :V7X_LANES] for x in wu]
    u_v = [x[:, V7X_LANES:] for x in wu]
    q_t = [r_t[:, sls[p]] + _mm(arb[p], stack(w_t[p])) for p in pairs]
    y_v = [_mm(jnp.concatenate([arb[p], ark[p]], axis=1),
               jnp.concatenate([stack(u_v[p]), vs[p]], axis=0)) for p in pairs]
    bp_t = [b_p[:, sl].T for sl in sls]
    kp_t = [k_p[:, sl].T for sl in sls]
    m_mat = [jnp.where(same_head, _mm(bp_t[p], w_t[p]), 0.0) for p in pairs]
    n_mat = [jnp.where(same_head, _mm(jnp.concatenate([bp_t[p], kp_t[p]], axis=1),
                                      jnp.concatenate([u_v[p], v_p[p]], axis=0)), 0.0) for p in pairs]
    z_old = [z_ref[p] for p in pairs]
    y_parts = [_mm(q_t[p], z_old[p]) + y_v[p] for p in pairs]
    for p in pairs:
        p_col = jnp.sum(jnp.where(eye, p_end[:, sls[p]], 0.0), axis=1, keepdims=True)
        z_ref[p] = p_col * z_old[p] + _mm(m_mat[p], z_old[p]) + n_mat[p]

    y = jnp.concatenate(y_parts, axis=1)
    inv_hd = 1.0 / HEAD_DIM
    mean = _mm_hilo(y, bones) * inv_hd
    d = y - mean
    var = _mm_hilo(d * d, bones) * inv_hd
    yn = d * lax.rsqrt(var + GN_EPS) * gn_w + gn_b
    bonus = _mm_hilo(r * k_mod * r_k, bones) * v
    o_ref[...] = ((yn + bonus) * g).astype(o_ref.dtype)


def _rwkv(r, k, v, l, prm, mul, du, iu, gu, bsz, t_len):
    n, aw = r.shape
    lw = l.shape[1]
    c = RWKV_CHUNK
    nt = t_len // c
    heads = np.arange(aw) // HEAD_DIM
    bones = jnp.asarray((heads[:, None] == heads[None, :]).astype(np.float32), BF16)
    lt = jnp.asarray(np.tril(np.ones((c, c), np.float32)), BF16)
    tok = lambda w: pl.BlockSpec((c, w), lambda b, t: (b * nt + t, 0))
    full = lambda arr: pl.BlockSpec(arr.shape, lambda b, t: (0,) * arr.ndim)
    return pl.pallas_call(
        _rwkv_body,
        grid=(bsz, nt),
        in_specs=[tok(aw), tok(aw), tok(aw), tok(lw), full(prm), full(mul), full(du), full(iu), full(gu),
                  full(bones), full(lt)],
        out_specs=tok(aw),
        out_shape=jax.ShapeDtypeStruct((n, aw), BF16),
        scratch_shapes=[pltpu.VMEM((aw // V7X_LANES, V7X_LANES, V7X_LANES), F32),
                        pltpu.VMEM((V7X_SUBLANES, 3 * aw + lw), F32)],
        compiler_params=_cparams(("parallel", "arbitrary")),
    )(r, k, v, l, prm, mul, du, iu, gu, bones, lt)


def _sb_body(q_ref, k_ref, v_ref, qg_ref, kg_ref, bo_ref, su_ref, o_ref, qm_ref, kn_ref, acc_ref, car_ref, *,
             kgroup, qgroup):
    t_len = q_ref.shape[0]
    blk = SB_BLOCK
    m0 = lax.broadcasted_iota(jnp.int32, (1, V7X_LANES), 1) < HEAD_DIM
    scale = LOG2_E / math.sqrt(HEAD_DIM)
    inv_hd = 1.0 / HEAD_DIM

    def prep(i, _):
        s0 = pl.multiple_of(i * blk, blk)
        q = q_ref[pl.ds(s0, blk), :].astype(F32)
        qn = q * lax.rsqrt(_mm_hilo(q * q, bo_ref[...]) * inv_hd + RMS_EPS) * (qg_ref[...] * scale)
        qm_ref[0, pl.ds(s0, blk), :] = jnp.where(m0, qn, 0.0).astype(BF16)
        qm_ref[1, pl.ds(s0, blk), :] = jnp.where(m0, 0.0, qn).astype(BF16)
        kx = k_ref[pl.ds(s0, blk), :].astype(F32)
        kn = kx * lax.rsqrt(_mm_hilo(kx * kx, bo_ref[...]) * inv_hd + RMS_EPS) * kg_ref[...]
        kn_ref[pl.ds(s0, blk), :] = kn.astype(BF16)
        return 0

    lax.fori_loop(0, t_len // blk, prep, 0)

    wide = kgroup * blk
    row = lax.broadcasted_iota(jnp.int32, (blk, wide), 0)
    col = lax.broadcasted_iota(jnp.int32, (blk, wide), 1)
    chains = [(qq, h) for qq in range(qgroup) for h in range(2)]

    def process(q0, k0, diag):
        kb = kn_ref[pl.ds(k0, wide), :]
        vb = v_ref[pl.ds(k0, wide), :]
        z = [_mm_nt(qm_ref[h, pl.ds(q0 + qq * blk, blk), :], kb) for qq, h in chains]
        sp = [jnp.maximum(zc, 0.0) + jnp.log2(1.0 + jnp.exp2(_neg_abs(zc))) for zc in z]
        if diag:
            causal = [(col + k0) < (row + q0 + qq * blk) for qq in range(qgroup)]
            sp = [jnp.where(causal[qq], x, 0.0) for (qq, _), x in zip(chains, sp)]
        hilo = [_split_hilo(x) for x in sp]
        su = su_ref[...]
        rs = [[jnp.dot(jnp.concatenate([hi[:, b * blk:(b + 1) * blk], lo[:, b * blk:(b + 1) * blk]], axis=1),
                       su, preferred_element_type=F32) for b in range(kgroup)] for hi, lo in hilo]
        att = []
        for c, (qq, _) in enumerate(chains):
            car = car_ref[c]
            after = [None] * kgroup
            for b in reversed(range(kgroup)):
                after[b] = rs[c][b][:, :blk] + car
                car = car + rs[c][b][:, blk:]
            car_ref[c] = car
            logit = z[c] - sp[c] - jnp.concatenate(after, axis=1)
            if diag:
                logit = jnp.where(causal[qq], logit, NEG_BIG)
            att.append(jnp.exp2(logit))
        for c in range(len(chains)):
            acc_ref[c] += _mm(att[c], vb)

    def qstep(i, _):
        q0 = pl.multiple_of(i * (qgroup * blk), qgroup * blk)
        acc_ref[...] = jnp.zeros_like(acc_ref)
        car_ref[...] = jnp.zeros_like(car_ref)
        sd = (i * qgroup + qgroup - 1) // kgroup
        process(q0, pl.multiple_of(sd * wide, wide), True)

        def kstep(jj, _):
            process(q0, pl.multiple_of((sd - 1 - jj) * wide, wide), False)
            return 0

        lax.fori_loop(0, sd, kstep, 0)
        for qq in range(qgroup):
            o_ref[pl.ds(q0 + qq * blk, blk), :] = jnp.where(m0, acc_ref[2 * qq], acc_ref[2 * qq + 1]
                                                            ).astype(o_ref.dtype)
        return 0

    lax.fori_loop(0, t_len // (qgroup * blk), qstep, 0)


def _stick_breaking(q, k, v, qg, kg, bsz, t_len):
    n, bw = q.shape
    n_pairs = bw // V7X_LANES
    blk = SB_BLOCK
    heads = np.arange(V7X_LANES) // HEAD_DIM
    bo = jnp.asarray((heads[:, None] == heads[None, :]).astype(np.float32), BF16)
    idx = np.arange(blk)
    su = np.concatenate([(idx[:, None] > idx[None, :]).astype(np.float32), np.ones((blk, blk), np.float32)], axis=1)
    su = jnp.asarray(np.concatenate([su, su], axis=0), BF16)
    tok = pl.BlockSpec((t_len, V7X_LANES), lambda b, p: (b, p))
    full = lambda arr: pl.BlockSpec(arr.shape, lambda b, p: (0,) * arr.ndim)
    kgroup = math.gcd(SB_KEY_GROUP, t_len // blk)
    qgroup = math.gcd(SB_QUERY_GROUP, kgroup)
    return pl.pallas_call(
        functools.partial(_sb_body, kgroup=kgroup, qgroup=qgroup),
        grid=(bsz, n_pairs),
        in_specs=[tok, tok, tok, full(qg), full(kg), full(bo), full(su)],
        out_specs=tok,
        out_shape=jax.ShapeDtypeStruct((n, bw), BF16),
        scratch_shapes=[pltpu.VMEM((2, t_len, V7X_LANES), BF16),
                        pltpu.VMEM((t_len, V7X_LANES), BF16),
                        pltpu.VMEM((2 * qgroup, blk, V7X_LANES), F32),
                        pltpu.VMEM((2 * qgroup, blk, blk), F32)],
        compiler_params=_cparams(("parallel", "parallel")),
    )(q, k, v, qg, kg, bo, su)


def _lru_body(gate_ref, rec_ref, cw_ref, pv_ref, wr_ref, wi_ref, o_ref, prev_ref, h_ref):
    tt = rec_ref.shape[0]
    width = rec_ref.shape[1]
    n_blocks = width // V7X_LANES

    @pl.when(pl.program_id(1) == 0)
    def _():
        prev_ref[...] = jnp.zeros_like(prev_ref)
        h_ref[...] = jnp.zeros_like(h_ref)

    row8 = lax.broadcasted_iota(jnp.int32, (V7X_SUBLANES, 1), 0)

    for gi in range(n_blocks):
        sl = slice(gi * V7X_LANES, (gi + 1) * V7X_LANES)
        u = rec_ref[:, sl].astype(F32)
        prev = prev_ref[:, sl]

        def shifted(s):
            ru = pltpu.roll(u, s, 0)
            first = jnp.where(row8 < s, pltpu.roll(prev, s, 0), ru[0:V7X_SUBLANES])
            return jnp.concatenate([first, ru[V7X_SUBLANES:]], axis=0)

        cw = cw_ref[:, sl]
        rc = (cw[3:4] * u + cw[2:3] * shifted(1) + cw[1:2] * shifted(2) + cw[0:1] * shifted(3)
              + pv_ref[0:1, sl])
        prev_ref[:, sl] = u[tt - V7X_SUBLANES:tt]

        rgate = _sigmoid(_mm(rc, wr_ref[gi]) + pv_ref[1:2, sl])
        igate = _sigmoid(_mm(rc, wi_ref[gi]) + pv_ref[2:3, sl])
        log_a = -LRU_C * rgate * _softplus(-pv_ref[3:4, sl])
        a = jnp.exp(log_a)
        om = 1.0 - a * a
        b = jnp.where(om > 0.0, om * lax.rsqrt(om), 0.0) * (igate * rc)

        s = 1
        while s < V7X_SUBLANES:
            ra = pltpu.roll(a, s, 0)
            rb = pltpu.roll(b, s, 0)
            a_sh = jnp.concatenate([jnp.where(row8 >= s, ra[0:V7X_SUBLANES], 1.0), ra[V7X_SUBLANES:]], axis=0)
            b_sh = jnp.concatenate([jnp.where(row8 >= s, rb[0:V7X_SUBLANES], 0.0), rb[V7X_SUBLANES:]], axis=0)
            b = a * b_sh + b
            a = a * a_sh
            s *= 2
        n_groups = tt // V7X_SUBLANES
        ag = [a[j * V7X_SUBLANES:(j + 1) * V7X_SUBLANES] for j in range(n_groups)]
        bg = [b[j * V7X_SUBLANES:(j + 1) * V7X_SUBLANES] for j in range(n_groups)]
        k = 1
        while k < n_groups:
            bg = [bg[j] if j < k else ag[j] * bg[j - k] + bg[j] for j in range(n_groups)]
            ag = [ag[j] if j < k else ag[j] * ag[j - k] for j in range(n_groups)]
            k *= 2
        h_in = h_ref[0:1, sl]
        h = jnp.concatenate([ag[j] * h_in + bg[j] for j in range(n_groups)], axis=0)
        h_ref[0:1, sl] = h[tt - 1:tt]
        o_ref[:, sl] = (jax.nn.gelu(gate_ref[:, sl].astype(F32)) * h).astype(o_ref.dtype)


def _rg_lru(gate, rec, cw, pv, wr, wi, bsz, t_len, tt):
    n, width = rec.shape
    nt = t_len // tt
    tok = pl.BlockSpec((tt, width), lambda b, t: (b * nt + t, 0))
    full = lambda arr: pl.BlockSpec(arr.shape, lambda b, t: (0,) * arr.ndim)
    return pl.pallas_call(
        _lru_body,
        grid=(bsz, nt),
        in_specs=[tok, tok, full(cw), full(pv), full(wr), full(wi)],
        out_specs=tok,
        out_shape=jax.ShapeDtypeStruct((n, width), BF16),
        scratch_shapes=[pltpu.VMEM((V7X_SUBLANES, width), F32), pltpu.VMEM((V7X_SUBLANES, width), F32)],
        compiler_params=_cparams(("parallel", "arbitrary")),
    )(gate, rec, cw, pv, wr, wi)


def _proj_norm_route_body(*refs, n_in):
    x_ref = refs[0]
    a_refs = refs[1:1 + n_in]
    w_refs = refs[1 + n_in:1 + 2 * n_in]
    g_ref, wh_ref, wl_ref, b_ref, slt_ref, xo_ref, h_ref, rt_ref, cnt_ref, run_ref = refs[1 + 2 * n_in:]

    @pl.when(pl.program_id(0) == 0)
    def _():
        run_ref[...] = jnp.zeros_like(run_ref)

    x = x_ref[...]
    for a_ref, w_ref in zip(a_refs, w_refs):
        x = x + jnp.dot(a_ref[...], w_ref[...], preferred_element_type=F32)
    xo_ref[...] = x
    ms = jnp.mean(x * x, axis=-1, keepdims=True)
    h = x * lax.rsqrt(ms + RMS_EPS) * g_ref[...]
    h_hi, h_lo = _split_hilo(h)
    h_ref[...] = h_hi
    wh = wh_ref[...]
    lg = (jnp.dot(h_hi, wh, preferred_element_type=F32) + jnp.dot(h_lo, wh, preferred_element_type=F32)
          + jnp.dot(h_hi, wl_ref[...], preferred_element_type=F32)) + b_ref[0:1]
    lane = lax.broadcasted_iota(jnp.int32, lg.shape, 1)
    neg_inf = -jnp.inf
    is_g = lane < N_GROUPS
    gl = jnp.where(is_g, lg, neg_inf)
    gmax = jnp.max(gl, axis=-1, keepdims=True)
    gidx = jnp.min(jnp.where(gl == gmax, lane, V7X_LANES), axis=-1, keepdims=True)
    g_top = 1.0 / jnp.sum(jnp.where(is_g, jnp.exp(gl - gmax), 0.0), axis=-1, keepdims=True)
    lo_e = N_GROUPS + EXPERTS_PER_GROUP * gidx
    el = jnp.where((lane >= lo_e) & (lane < lo_e + EXPERTS_PER_GROUP), lg, neg_inf)
    m1 = jnp.max(el, axis=-1, keepdims=True)
    i1 = jnp.min(jnp.where(el == m1, lane, V7X_LANES), axis=-1, keepdims=True)
    el2 = jnp.where(lane == i1, neg_inf, el)
    m2 = jnp.max(el2, axis=-1, keepdims=True)
    i2 = jnp.min(jnp.where(el2 == m2, lane, V7X_LANES), axis=-1, keepdims=True)
    t = jnp.exp(m2 - m1)
    gate1 = g_top / (1.0 + t)
    gate2 = g_top * t / (1.0 + t)
    oh1 = lane == i1
    oh2 = lane == i2
    oh = jnp.where(oh1 | oh2, 1.0, 0.0)
    before = jnp.dot(slt_ref[...], oh.astype(BF16), preferred_element_type=F32) + run_ref[0:1]
    rank1 = jnp.sum(jnp.where(oh1, before, 0.0), axis=-1, keepdims=True)
    rank2 = jnp.sum(jnp.where(oh2, before, 0.0), axis=-1, keepdims=True)
    run_ref[0:1] = run_ref[0:1] + jnp.sum(oh, axis=0, keepdims=True)
    cnt_ref[...] = jnp.broadcast_to(run_ref[0:1], cnt_ref.shape)
    vals = (i1 - N_GROUPS, i2 - N_GROUPS, rank1, rank2, gate1, gate2)
    out = jnp.zeros(lg.shape, F32)
    for li, val in enumerate(vals):
        out = jnp.where(lane == li, val.astype(F32), out)
    rt_ref[...] = out.T[0:V7X_SUBLANES]


def _proj_norm_route(x, acts, ws, g, w_router, b_router, tm):
    n, d = x.shape
    n_in = len(acts)
    nr = w_router.shape[1]
    wpad = jnp.zeros((d, V7X_LANES), F32).at[:, :nr].set(w_router)
    wh = wpad.astype(BF16)
    wl = (wpad - wh.astype(F32)).astype(BF16)
    bias = jnp.zeros((V7X_SUBLANES, V7X_LANES), F32).at[0, :nr].set(b_router)
    slt = jnp.asarray(np.tril(np.ones((tm, tm), np.float32), -1), BF16)
    const = lambda shape: pl.BlockSpec(shape, lambda i: (0, 0))
    rows = lambda width: pl.BlockSpec((tm, width), lambda i: (i, 0))
    return pl.pallas_call(
        functools.partial(_proj_norm_route_body, n_in=n_in),
        grid=(n // tm,),
        in_specs=([rows(d)] + [rows(a.shape[1]) for a in acts] + [const(w.shape) for w in ws]
                  + [const((1, d)), const((d, V7X_LANES)), const((d, V7X_LANES)),
                     const((V7X_SUBLANES, V7X_LANES)), const((tm, tm))]),
        out_specs=[rows(d), rows(d), pl.BlockSpec((V7X_SUBLANES, tm), lambda i: (0, i)),
                   const((V7X_SUBLANES, V7X_LANES))],
        out_shape=[jax.ShapeDtypeStruct((n, d), F32), jax.ShapeDtypeStruct((n, d), BF16),
                   jax.ShapeDtypeStruct((V7X_SUBLANES, n), F32),
                   jax.ShapeDtypeStruct((V7X_SUBLANES, V7X_LANES), F32)],
        scratch_shapes=[pltpu.VMEM((V7X_SUBLANES, V7X_LANES), F32)],
        compiler_params=_cparams(("arbitrary",)),
    )(x, *acts, *ws, g, wh, wl, bias, slt)


def _moe_body(te_ref, nv_ref, x_ref, wg_ref, wu_ref, wd_ref, o_ref, wgb_ref, wub_ref, wdb_ref):
    i = pl.program_id(0)
    valid = i < nv_ref[0]

    @pl.when(valid & ((i == 0) | (te_ref[i] != te_ref[jnp.maximum(i - 1, 0)])))
    def _():
        wgb_ref[...] = wg_ref[0].astype(BF16)
        wub_ref[...] = wu_ref[0].astype(BF16)
        wdb_ref[...] = wd_ref[0].astype(BF16)

    @pl.when(valid)
    def _():
        x = x_ref[...]
        gt = jnp.dot(x, wgb_ref[...], preferred_element_type=F32)
        up = jnp.dot(x, wub_ref[...], preferred_element_type=F32)
        hid = (gt * _sigmoid(gt)) * up
        o_ref[...] = jnp.dot(hid.astype(BF16), wdb_ref[...], preferred_element_type=F32).astype(o_ref.dtype)


def _moe_grouped(xs, tile_expert, n_valid, wg, wu, wd, layer):
    p_rows, d = xs.shape
    de = wg.shape[3]
    tm = MOE_TILE
    grid_spec = pltpu.PrefetchScalarGridSpec(
        num_scalar_prefetch=2,
        grid=(p_rows // tm,),
        in_specs=[pl.BlockSpec((tm, d), lambda i, te, nv: (i, 0)),
                  pl.BlockSpec((None, 1, d, de), lambda i, te, nv: (layer, te[i], 0, 0)),
                  pl.BlockSpec((None, 1, d, de), lambda i, te, nv: (layer, te[i], 0, 0)),
                  pl.BlockSpec((None, 1, de, d), lambda i, te, nv: (layer, te[i], 0, 0))],
        out_specs=pl.BlockSpec((tm, d), lambda i, te, nv: (i, 0)),
        scratch_shapes=[pltpu.VMEM((d, de), BF16), pltpu.VMEM((d, de), BF16), pltpu.VMEM((de, d), BF16)],
    )
    return pl.pallas_call(
        _moe_body,
        grid_spec=grid_spec,
        out_shape=jax.ShapeDtypeStruct((p_rows, d), BF16),
        compiler_params=_cparams(("arbitrary",)),
    )(tile_expert, n_valid, xs, wg, wu, wd)


def _mix_out_moe(x, acts, ws, g_norm, w_group, b_group, w_erouter, b_erouter, wg, wu, wd, layer, tm):
    n, d = x.shape
    x, hn, route, cnt = _proj_norm_route(x, acts, ws, g_norm, jnp.concatenate([w_group, w_erouter], axis=1),
                                         jnp.concatenate([b_group, b_erouter]), tm)
    expert_id = route[0:2].astype(jnp.int32)
    rank = route[2:4].astype(jnp.int32)
    gates = route[4:6]
    counts = cnt[0, N_GROUPS:N_GROUPS + N_EXPERTS].astype(jnp.int32)

    tmm = MOE_TILE
    padded = ((counts + tmm - 1) // tmm) * tmm
    ends = jnp.cumsum(padded)
    starts = ends - padded
    pos = starts[expert_id] + rank
    p_rows = 2 * n + N_EXPERTS * tmm
    n_tiles = p_rows // tmm
    tok = jnp.broadcast_to(jnp.arange(n, dtype=jnp.int32)[None, :], (2, n))
    row_token = jnp.zeros((p_rows,), jnp.int32).at[pos.reshape(-1)].set(
        tok.reshape(-1), mode="promise_in_bounds", unique_indices=True)
    tile_expert = jnp.minimum(
        jnp.searchsorted(ends, jnp.arange(n_tiles, dtype=jnp.int32) * tmm, side="right"), N_EXPERTS - 1
    ).astype(jnp.int32)
    n_valid = (ends[-1] // tmm).astype(jnp.int32).reshape(1)

    take_rows = lambda a, idx: a.at[idx].get(mode="promise_in_bounds")
    xs = take_rows(hn, row_token)
    ys = _moe_grouped(xs, tile_expert, n_valid, wg, wu, wd, layer)
    y = (gates[0][:, None] * take_rows(ys, pos[0]).astype(F32)
         + gates[1][:, None] * take_rows(ys, pos[1]).astype(F32))
    return x + y


def _pad_rows(w, rows, offset=0):
    out = jnp.zeros((rows, w.shape[1]), w.dtype)
    return out.at[offset:offset + w.shape[0]].set(w)


def kernel(x, norm_mix, norm_ffn, w_in_even, mu_a, w0, decay_up, a0, iclr_up, gate_up, k_k, k_a, r_k, gn_w,
           gn_b, q_norm_g, k_norm_g, w_out_even, w_in_odd, conv_w, conv_b, w_rgate, b_rgate, w_igate,
           b_igate, lru_lambda, w_out_odd, w_group, b_group, w_erouter, b_erouter, exp_w_gate, exp_w_up,
           exp_w_down):
    bsz, t_len, d = x.shape
    n = bsz * t_len
    depth = norm_mix.shape[0]
    aw = decay_up.shape[2]
    d_lora, i_lora, g_lora = decay_up.shape[1], iclr_up.shape[1], gate_up.shape[1]
    bw = (w_in_even.shape[2] - (3 * aw + d_lora + i_lora + g_lora)) // 3
    lw = w_rgate.shape[1] * w_rgate.shape[2]
    assert d_lora + i_lora == V7X_LANES and g_lora <= 2 * V7X_LANES
    lora_w = 3 * V7X_LANES
    tm = 512
    xf = x.reshape(n, d)

    for layer in range(depth):
        gm = norm_mix[layer].reshape(1, d)
        if layer % 2 == 0:
            i = layer // 2
            w_in = w_in_even[i]
            o3 = 3 * aw
            a_cols = o3 + d_lora + i_lora + g_lora
            pad = jnp.zeros((d, lora_w - (d_lora + i_lora + g_lora)), F32)
            w_cat = jnp.concatenate([w_in[:, :a_cols], pad, w_in[:, a_cols:]], axis=1).astype(BF16)
            r, k, v, lo, qb, kb, vb = _norm_matmul(xf, gm, w_cat, (aw, aw, aw, lora_w, bw, bw, bw), tm)
            mu = mu_a[i]
            rows = [mu[:aw], mu[aw:2 * aw], mu[2 * aw:o3], w0[i], a0[i], k_k[i], k_a[i], r_k[i].reshape(-1),
                    gn_w[i], gn_b[i]]
            prm = jnp.zeros((16, aw), F32).at[:len(rows)].set(jnp.stack(rows))
            mul = jnp.zeros((V7X_SUBLANES, lora_w), F32).at[0, :a_cols - o3].set(mu[o3:])
            du = _pad_rows(decay_up[i], V7X_LANES, 0).astype(BF16)
            iu = _pad_rows(iclr_up[i], V7X_LANES, d_lora).astype(BF16)
            gu = _pad_rows(gate_up[i], 2 * V7X_LANES, 0).astype(BF16)
            ya = _rwkv(r, k, v, lo, prm, mul, du, iu, gu, bsz, t_len)
            qg = jnp.tile(q_norm_g[i], V7X_LANES // HEAD_DIM).reshape(1, V7X_LANES)
            kg = jnp.tile(k_norm_g[i], V7X_LANES // HEAD_DIM).reshape(1, V7X_LANES)
            yb = _stick_breaking(qb, kb, vb, qg, kg, bsz, t_len)
            w_out = w_out_even[i].astype(BF16)
            acts, ws = [ya, yb], [w_out[:aw], w_out[aw:]]
        else:
            j = layer // 2
            gate, rec = _norm_matmul(xf, gm, w_in_odd[j].astype(BF16), (lw, lw), tm)
            cw = jnp.zeros((V7X_SUBLANES, lw), F32).at[:conv_w.shape[1]].set(conv_w[j])
            pv = jnp.zeros((V7X_SUBLANES, lw), F32).at[:4].set(
                jnp.stack([conv_b[j], b_rgate[j], b_igate[j], lru_lambda[j]]))
            yl = _rg_lru(gate, rec, cw, pv, w_rgate[j].astype(BF16), w_igate[j].astype(BF16), bsz, t_len,
                         min(256, t_len))
            acts, ws = [yl], [w_out_odd[j].astype(BF16)]
        xf = _mix_out_moe(xf, acts, ws, norm_ffn[layer].reshape(1, d), w_group[layer], b_group[layer],
                          w_erouter[layer], b_erouter[layer], exp_w_gate, exp_w_up, exp_w_down, layer, tm)
    return xf.reshape(bsz, t_len, d)
```

```python
import functools
import math

import numpy as np
import jax
import jax.numpy as jnp
from jax import lax
from jax.experimental import pallas as pl
from jax.experimental.pallas import tpu as pltpu

F32 = jnp.float32
BF16 = jnp.bfloat16

V7X_LANES = 128
V7X_SUBLANES = 8
V7X_VMEM_LIMIT_BYTES = 48 * 1024 * 1024

HEAD_DIM = 64
RMS_EPS = 1e-6
GN_EPS = 64e-5
LRU_C = 8.0
N_GROUPS = 4
EXPERTS_PER_GROUP = 4
N_EXPERTS = 16
RWKV_CHUNK = 128
SB_BLOCK = 128
SB_KEY_GROUP = 4
SB_QUERY_GROUP = 2
MOE_TILE = 512
NEG_BIG = -1e30
LOG2_E = 1.4426950408889634


def _cparams(sem):
    return pltpu.CompilerParams(dimension_semantics=sem, vmem_limit_bytes=V7X_VMEM_LIMIT_BYTES)


def _mm(a, b):
    return jnp.dot(a.astype(BF16), b.astype(BF16), preferred_element_type=F32)


def _mm_nt(a, b):
    return lax.dot_general(a.astype(BF16), b.astype(BF16), (((1,), (1,)), ((), ())),
                           preferred_element_type=F32)


def _split_hilo(x):
    hi = x.astype(BF16)
    lo = (x - hi.astype(F32)).astype(BF16)
    return hi, lo


def _mm_hilo(x, m):
    hi, lo = _split_hilo(x)
    return (jnp.dot(hi, m, preferred_element_type=F32) + jnp.dot(lo, m, preferred_element_type=F32))


def _hilo_mm(m, x):
    hi, lo = _split_hilo(x)
    return (jnp.dot(m, hi, preferred_element_type=F32) + jnp.dot(m, lo, preferred_element_type=F32))


def _sigmoid(x):
    return 0.5 * jnp.tanh(0.5 * x) + 0.5


def _neg_abs(x):
    bits = pltpu.bitcast(x, jnp.uint32) | jnp.uint32(0x80000000)
    return pltpu.bitcast(bits, F32)


def _softplus(x):
    return jnp.maximum(x, 0.0) + jnp.log(1.0 + jnp.exp(-jnp.abs(x)))


def _norm_matmul_body(x_ref, g_ref, w_ref, *out_refs, col_splits):
    x = x_ref[...]
    ms = jnp.mean(x * x, axis=-1, keepdims=True)
    h = (x * lax.rsqrt(ms + RMS_EPS) * g_ref[...]).astype(BF16)
    off = 0
    for o_ref, c in zip(out_refs, col_splits):
        o_ref[...] = jnp.dot(h, w_ref[:, off:off + c], preferred_element_type=F32).astype(o_ref.dtype)
        off += c


def _norm_matmul(x, g, w, col_splits, tm):
    n, d = x.shape
    ctot = w.shape[1]
    assert sum(col_splits) == ctot and n % tm == 0
    return pl.pallas_call(
        functools.partial(_norm_matmul_body, col_splits=tuple(col_splits)),
        grid=(n // tm,),
        in_specs=[pl.BlockSpec((tm, d), lambda i: (i, 0)),
                  pl.BlockSpec((1, d), lambda i: (0, 0)),
                  pl.BlockSpec((d, ctot), lambda i: (0, 0))],
        out_specs=[pl.BlockSpec((tm, c), lambda i: (i, 0)) for c in col_splits],
        out_shape=[jax.ShapeDtypeStruct((n, c), BF16) for c in col_splits],
        compiler_params=_cparams(("parallel",)),
    )(x, g, w)


def _tri_inverse_many(mats, row, col):
    eye = (row == col).astype(F32)
    blk8 = (row // 8) == (col // 8)
    n1 = [jnp.where(blk8, a, 0.0) for a in mats]
    n2 = [_mm(n, n) for n in n1]
    ipn = [eye + n for n in n1]
    p = [i + _mm(i, m) for i, m in zip(ipn, n2)]
    n4 = [_mm(m, m) for m in n2]
    x = [q + _mm(q, m) for q, m in zip(p, n4)]
    k = 8
    c = mats[0].shape[0]
    while k < c:
        sel = ((row // (2 * k)) == (col // (2 * k))) & ((row // k) > (col // k))
        e = [jnp.where(sel, a, 0.0) for a in mats]
        xe = [_mm(xi, ei) for xi, ei in zip(x, e)]
        x = [xi + _mm(xei, xi) for xi, xei in zip(x, xe)]
        k *= 2
    return x


def _rwkv_body(r_ref, k_ref, v_ref, l_ref, p_ref, mul_ref, du_ref, iu_ref, gu_ref, bones_ref, lt_ref,
               o_ref, z_ref, carry_ref):
    c = RWKV_CHUNK
    aw = r_ref.shape[1]
    lw = l_ref.shape[1]
    n_pairs = aw // V7X_LANES

    @pl.when(pl.program_id(1) == 0)
    def _():
        z_ref[...] = jnp.zeros_like(z_ref)
        carry_ref[...] = jnp.zeros_like(carry_ref)

    row1 = lax.broadcasted_iota(jnp.int32, (c, 1), 0)

    def token_shift(raw, prev_last, mu):
        prev = jnp.where(row1 == 0, prev_last, pltpu.roll(raw, 1, 0))
        return raw + mu * (prev - raw)

    prm = p_ref[...]
    mu_r, mu_k, mu_v, w0, a0, k_k, k_a, r_k, gn_w, gn_b = (prm[i:i + 1] for i in range(10))

    r_raw = r_ref[...].astype(F32)
    k_raw = k_ref[...].astype(F32)
    v_raw = v_ref[...].astype(F32)
    l_raw = l_ref[...].astype(F32)
    r = token_shift(r_raw, carry_ref[0:1, 0:aw], mu_r)
    k = token_shift(k_raw, carry_ref[0:1, aw:2 * aw], mu_k)
    v = token_shift(v_raw, carry_ref[0:1, 2 * aw:3 * aw], mu_v)
    xl = token_shift(l_raw, carry_ref[0:1, 3 * aw:3 * aw + lw], mul_ref[0:1])
    carry_ref[0:1, 0:aw] = r_raw[c - 1:c]
    carry_ref[0:1, aw:2 * aw] = k_raw[c - 1:c]
    carry_ref[0:1, 2 * aw:3 * aw] = v_raw[c - 1:c]
    carry_ref[0:1, 3 * aw:3 * aw + lw] = l_raw[c - 1:c]

    l0 = xl[:, :V7X_LANES]
    l1 = xl[:, V7X_LANES:]
    dw = _mm(jnp.tanh(l0), du_ref[...])
    da = _mm(l0, iu_ref[...])
    g = _mm(_sigmoid(l1), gu_ref[...])
    w_log = -_softplus(-(w0 + dw)) - 0.5
    logw = -jnp.exp(w_log)
    a = _sigmoid(a0 + da)

    bones = bones_ref[...]
    kk = k * k_k
    ssq = _mm_hilo(kk * kk, bones)
    kk = kk * jnp.minimum(lax.rsqrt(ssq), 1e12)
    k_mod = k * (1.0 + (a - 1.0) * k_a)
    a_vec = -kk
    b_vec = kk * a

    cum = _hilo_mm(lt_ref[...], logw)
    cmid = cum[c // 2 - 1:c // 2]
    cend = cum[c - 1:c]
    emid = jnp.exp(cmid)
    a_s = a_vec * jnp.exp(cum - logw - cmid)
    a_t = a_s * emid
    r_s = r * jnp.exp(cum - cmid)
    r_t = r_s * emid
    e_neg = jnp.exp(cmid - cum)
    b_s = b_vec * e_neg
    k_s = k_mod * e_neg
    e_end = jnp.exp(cend - cum)
    b_p = b_vec * e_end
    k_p = k_mod * e_end
    p_end = jnp.exp(cend)

    row = lax.broadcasted_iota(jnp.int32, (c, c), 0)
    col = lax.broadcasted_iota(jnp.int32, (c, c), 1)
    strict = col < row
    incl = col <= row
    eye = row == col
    same_head = (row // HEAD_DIM) == (col // HEAD_DIM)
    m0 = lax.broadcasted_iota(jnp.int32, (1, V7X_LANES), 1) < HEAD_DIM

    def stack(x):
        return jnp.concatenate([jnp.where(m0, x, 0.0), jnp.where(m0, 0.0, x)], axis=0)

    pairs = range(n_pairs)
    sls = [slice(p * V7X_LANES, (p + 1) * V7X_LANES) for p in pairs]
    v_p = [v[:, sl] for sl in sls]
    gm = [_mm_nt(jnp.concatenate([stack(a_s[:, sl]), stack(r_s[:, sl])], axis=0),
                 jnp.concatenate([b_s[:, sl], k_s[:, sl]], axis=0)) for sl in sls]
    aab = [jnp.where(strict, gm[p][h * c:(h + 1) * c, 0:c], 0.0) for p in pairs for h in range(2)]
    tinv = _tri_inverse_many(aab, row, col)
    aak = [jnp.concatenate([jnp.where(strict, gm[p][h * c:(h + 1) * c, c:2 * c], 0.0) for h in range(2)],
                           axis=1) for p in pairs]
    arb = [jnp.concatenate([jnp.where(incl, gm[p][(2 + h) * c:(3 + h) * c, 0:c], 0.0) for h in range(2)],
                           axis=1) for p in pairs]
    ark = [jnp.concatenate([jnp.where(incl, gm[p][(2 + h) * c:(3 + h) * c, c:2 * c], 0.0) for h in range(2)],
                           axis=1) for p in pairs]
    vs = [stack(x) for x in v_p]
    akv = [_mm(aak[p], vs[p]) for p in pairs]
    wu = [_mm(jnp.concatenate([tinv[2 * p], tinv[2 * p + 1]], axis=1),
              jnp.concatenate([stack(a_t[:, sls[p]]), stack(akv[p])], axis=1)) for p in pairs]
    w_t = [x[:, :V7X_LANES] for x in wu]
    u_v = [x[:, V7X_LANES:] for x in wu]
    q_t = [r_t[:, sls[p]] + _mm(arb[p], stack(w_t[p])) for p in pairs]
    y_v = [_mm(jnp.concatenate([arb[p], ark[p]], axis=1),
               jnp.concatenate([stack(u_v[p]), vs[p]], axis=0)) for p in pairs]
    bp_t = [b_p[:, sl].T for sl in sls]
    kp_t = [k_p[:, sl].T for sl in sls]
    m_mat = [jnp.where(same_head, _mm(bp_t[p], w_t[p]), 0.0) for p in pairs]
    n_mat = [jnp.where(same_head, _mm(jnp.concatenate([bp_t[p], kp_t[p]], axis=1),
                                      jnp.concatenate([u_v[p], v_p[p]], axis=0)), 0.0) for p in pairs]
    z_old = [z_ref[p] for p in pairs]
    y_parts = [_mm(q_t[p], z_old[p]) + y_v[p] for p in pairs]
    for p in pairs:
        p_col = jnp.sum(jnp.where(eye, p_end[:, sls[p]], 0.0), axis=1, keepdims=True)
        z_ref[p] = p_col * z_old[p] + _mm(m_mat[p], z_old[p]) + n_mat[p]

    y = jnp.concatenate(y_parts, axis=1)
    inv_hd = 1.0 / HEAD_DIM
    mean = _mm_hilo(y, bones) * inv_hd
    d = y - mean
    var = _mm_hilo(d * d, bones) * inv_hd
    yn = d * lax.rsqrt(var + GN_EPS) * gn_w + gn_b
    bonus = _mm_hilo(r * k_mod * r_k, bones) * v
    o_ref[...] = ((yn + bonus) * g).astype(o_ref.dtype)


def _rwkv(r, k, v, l, prm, mul, du, iu, gu, bsz, t_len):
    n, aw = r.shape
    lw = l.shape[1]
    c = RWKV_CHUNK
    nt = t_len // c
    heads = np.arange(aw) // HEAD_DIM
    bones = jnp.asarray((heads[:, None] == heads[None, :]).astype(np.float32), BF16)
    lt = jnp.asarray(np.tril(np.ones((c, c), np.float32)), BF16)
    tok = lambda w: pl.BlockSpec((c, w), lambda b, t: (b * nt + t, 0))
    full = lambda arr: pl.BlockSpec(arr.shape, lambda b, t: (0,) * arr.ndim)
    return pl.pallas_call(
        _rwkv_body,
        grid=(bsz, nt),
        in_specs=[tok(aw), tok(aw), tok(aw), tok(lw), full(prm), full(mul), full(du), full(iu), full(gu),
                  full(bones), full(lt)],
        out_specs=tok(aw),
        out_shape=jax.ShapeDtypeStruct((n, aw), BF16),
        scratch_shapes=[pltpu.VMEM((aw // V7X_LANES, V7X_LANES, V7X_LANES), F32),
                        pltpu.VMEM((V7X_SUBLANES, 3 * aw + lw), F32)],
        compiler_params=_cparams(("parallel", "arbitrary")),
    )(r, k, v, l, prm, mul, du, iu, gu, bones, lt)


def _sb_body(q_ref, k_ref, v_ref, qg_ref, kg_ref, bo_ref, su_ref, o_ref, qm_ref, kn_ref, acc_ref, car_ref):
    t_len = q_ref.shape[0]
    blk = SB_BLOCK
    m0 = lax.broadcasted_iota(jnp.int32, (1, V7X_LANES), 1) < HEAD_DIM
    scale = LOG2_E / math.sqrt(HEAD_DIM)
    inv_hd = 1.0 / HEAD_DIM

    def prep(i, _):
        s0 = pl.multiple_of(i * blk, blk)
        q = q_ref[pl.ds(s0, blk), :].astype(F32)
        qn = q * lax.rsqrt(_mm_hilo(q * q, bo_ref[...]) * inv_hd + RMS_EPS) * (qg_ref[...] * scale)
        qm_ref[0, pl.ds(s0, blk), :] = jnp.where(m0, qn, 0.0).astype(BF16)
        qm_ref[1, pl.ds(s0, blk), :] = jnp.where(m0, 0.0, qn).astype(BF16)
        kx = k_ref[pl.ds(s0, blk), :].astype(F32)
        kn = kx * lax.rsqrt(_mm_hilo(kx * kx, bo_ref[...]) * inv_hd + RMS_EPS) * kg_ref[...]
        kn_ref[pl.ds(s0, blk), :] = kn.astype(BF16)
        return 0

    lax.fori_loop(0, t_len // blk, prep, 0)

    n_qblocks = t_len // blk
    chains = [(qq, h) for qq in range(SB_QUERY_GROUP) for h in range(2)]

    def process(q0, k0, n_kb, diag):
        width = n_kb * blk
        kb = kn_ref[pl.ds(k0, width), :]
        vb = v_ref[pl.ds(k0, width), :]
        z = [_mm_nt(qm_ref[h, pl.ds(q0 + qq * blk, blk), :], kb) for qq, h in chains]
        sp = [jnp.maximum(zc, 0.0) + jnp.log2(1.0 + jnp.exp2(_neg_abs(zc))) for zc in z]
        if diag:
            row = lax.broadcasted_iota(jnp.int32, (blk, width), 0)
            col = lax.broadcasted_iota(jnp.int32, (blk, width), 1)
            causal = [(col + k0) < (row + q0 + qq * blk) for qq in range(SB_QUERY_GROUP)]
            sp = [jnp.where(causal[qq], x, 0.0) for (qq, _), x in zip(chains, sp)]
        sp_lp = [x.astype(BF16) for x in sp]
        su = su_ref[...]
        rs = [[jnp.dot(x[:, b * blk:(b + 1) * blk], su, preferred_element_type=F32) for b in range(n_kb)]
              for x in sp_lp]
        att = []
        for c, (qq, _) in enumerate(chains):
            car = car_ref[c]
            after = [None] * n_kb
            for b in reversed(range(n_kb)):
                after[b] = rs[c][b][:, :blk] + car
                car = car + rs[c][b][:, blk:]
            car_ref[c] = car
            logit = z[c] - sp[c] - jnp.concatenate(after, axis=1)
            if diag:
                logit = jnp.where(causal[qq], logit, NEG_BIG)
            att.append(jnp.exp2(logit))
        for c in range(len(chains)):
            acc_ref[c] += _mm(att[c], vb)

    def sweep_pair(q0, n_wide, half_step):
        acc_ref[...] = jnp.zeros_like(acc_ref)
        car_ref[...] = jnp.zeros_like(car_ref)
        process(q0, q0, SB_QUERY_GROUP, True)
        base = q0
        if half_step:
            base = q0 - SB_QUERY_GROUP * blk
            process(q0, base, SB_QUERY_GROUP, False)
        wide = SB_KEY_GROUP * blk

        def kstep(jj, _):
            process(q0, pl.multiple_of(base - (jj + 1) * wide, wide), SB_KEY_GROUP, False)
            return 0

        lax.fori_loop(0, n_wide, kstep, 0)
        for qq in range(SB_QUERY_GROUP):
            o_ref[pl.ds(q0 + qq * blk, blk), :] = jnp.where(m0, acc_ref[2 * qq], acc_ref[2 * qq + 1]
                                                            ).astype(o_ref.dtype)

    quad = SB_KEY_GROUP * blk
    n_quads = n_qblocks // SB_KEY_GROUP

    def quad_step(m, _):
        q0 = pl.multiple_of(m * quad, quad)
        sweep_pair(q0, m, False)
        sweep_pair(q0 + SB_QUERY_GROUP * blk, m, True)
        return 0

    lax.fori_loop(0, n_quads, quad_step, 0)
    if n_qblocks % SB_KEY_GROUP:
        sweep_pair(n_quads * quad, n_quads, False)


def _stick_breaking(q, k, v, qg, kg, bsz, t_len):
    n, bw = q.shape
    n_pairs = bw // V7X_LANES
    blk = SB_BLOCK
    heads = np.arange(V7X_LANES) // HEAD_DIM
    bo = jnp.asarray((heads[:, None] == heads[None, :]).astype(np.float32), BF16)
    idx = np.arange(blk)
    su = np.concatenate([(idx[:, None] > idx[None, :]).astype(np.float32), np.ones((blk, blk), np.float32)], axis=1)
    su = jnp.asarray(su, BF16)
    tok = pl.BlockSpec((t_len, V7X_LANES), lambda b, p: (b, p))
    full = lambda arr: pl.BlockSpec(arr.shape, lambda b, p: (0,) * arr.ndim)
    assert SB_KEY_GROUP == 2 * SB_QUERY_GROUP and t_len % (SB_QUERY_GROUP * blk) == 0
    qgroup = SB_QUERY_GROUP
    return pl.pallas_call(
        _sb_body,
        grid=(bsz, n_pairs),
        in_specs=[tok, tok, tok, full(qg), full(kg), full(bo), full(su)],
        out_specs=tok,
        out_shape=jax.ShapeDtypeStruct((n, bw), BF16),
        scratch_shapes=[pltpu.VMEM((2, t_len, V7X_LANES), BF16),
                        pltpu.VMEM((t_len, V7X_LANES), BF16),
                        pltpu.VMEM((2 * qgroup, blk, V7X_LANES), F32),
                        pltpu.VMEM((2 * qgroup, blk, blk), F32)],
        compiler_params=_cparams(("parallel", "parallel")),
    )(q, k, v, qg, kg, bo, su)


def _lru_body(gate_ref, rec_ref, cw_ref, pv_ref, wr_ref, wi_ref, o_ref, prev_ref, h_ref):
    tt = rec_ref.shape[0]
    width = rec_ref.shape[1]
    n_blocks = width // V7X_LANES

    @pl.when(pl.program_id(1) == 0)
    def _():
        prev_ref[...] = jnp.zeros_like(prev_ref)
        h_ref[...] = jnp.zeros_like(h_ref)

    row8 = lax.broadcasted_iota(jnp.int32, (V7X_SUBLANES, 1), 0)

    for gi in range(n_blocks):
        sl = slice(gi * V7X_LANES, (gi + 1) * V7X_LANES)
        u = rec_ref[:, sl].astype(F32)
        prev = prev_ref[:, sl]

        def shifted(s):
            ru = pltpu.roll(u, s, 0)
            first = jnp.where(row8 < s, pltpu.roll(prev, s, 0), ru[0:V7X_SUBLANES])
            return jnp.concatenate([first, ru[V7X_SUBLANES:]], axis=0)

        cw = cw_ref[:, sl]
        rc = (cw[3:4] * u + cw[2:3] * shifted(1) + cw[1:2] * shifted(2) + cw[0:1] * shifted(3)
              + pv_ref[0:1, sl])
        prev_ref[:, sl] = u[tt - V7X_SUBLANES:tt]

        rgate = _sigmoid(_mm(rc, wr_ref[gi]) + pv_ref[1:2, sl])
        igate = _sigmoid(_mm(rc, wi_ref[gi]) + pv_ref[2:3, sl])
        log_a = -LRU_C * rgate * _softplus(-pv_ref[3:4, sl])
        a = jnp.exp(log_a)
        om = 1.0 - a * a
        b = jnp.where(om > 0.0, om * lax.rsqrt(om), 0.0) * (igate * rc)

        s = 1
        while s < V7X_SUBLANES:
            ra = pltpu.roll(a, s, 0)
            rb = pltpu.roll(b, s, 0)
            a_sh = jnp.concatenate([jnp.where(row8 >= s, ra[0:V7X_SUBLANES], 1.0), ra[V7X_SUBLANES:]], axis=0)
            b_sh = jnp.concatenate([jnp.where(row8 >= s, rb[0:V7X_SUBLANES], 0.0), rb[V7X_SUBLANES:]], axis=0)
            b = a * b_sh + b
            a = a * a_sh
            s *= 2
        n_groups = tt // V7X_SUBLANES
        ag = [a[j * V7X_SUBLANES:(j + 1) * V7X_SUBLANES] for j in range(n_groups)]
        bg = [b[j * V7X_SUBLANES:(j + 1) * V7X_SUBLANES] for j in range(n_groups)]
        k = 1
        while k < n_groups:
            bg = [bg[j] if j < k else ag[j] * bg[j - k] + bg[j] for j in range(n_groups)]
            ag = [ag[j] if j < k else ag[j] * ag[j - k] for j in range(n_groups)]
            k *= 2
        h_in = h_ref[0:1, sl]
        h = jnp.concatenate([ag[j] * h_in + bg[j] for j in range(n_groups)], axis=0)
        h_ref[0:1, sl] = h[tt - 1:tt]
        o_ref[:, sl] = (jax.nn.gelu(gate_ref[:, sl].astype(F32)) * h).astype(o_ref.dtype)


def _rg_lru(gate, rec, cw, pv, wr, wi, bsz, t_len, tt):
    n, width = rec.shape
    nt = t_len // tt
    tok = pl.BlockSpec((tt, width), lambda b, t: (b * nt + t, 0))
    full = lambda arr: pl.BlockSpec(arr.shape, lambda b, t: (0,) * arr.ndim)
    return pl.pallas_call(
        _lru_body,
        grid=(bsz, nt),
        in_specs=[tok, tok, full(cw), full(pv), full(wr), full(wi)],
        out_specs=tok,
        out_shape=jax.ShapeDtypeStruct((n, width), BF16),
        scratch_shapes=[pltpu.VMEM((V7X_SUBLANES, width), F32), pltpu.VMEM((V7X_SUBLANES, width), F32)],
        compiler_params=_cparams(("parallel", "arbitrary")),
    )(gate, rec, cw, pv, wr, wi)


def _proj_norm_route_body(*refs, n_in):
    x_ref = refs[0]
    a_refs = refs[1:1 + n_in]
    w_refs = refs[1 + n_in:1 + 2 * n_in]
    g_ref, wh_ref, wl_ref, b_ref, slt_ref, xo_ref, h_ref, rt_ref, cnt_ref, run_ref = refs[1 + 2 * n_in:]

    @pl.when(pl.program_id(0) == 0)
    def _():
        run_ref[...] = jnp.zeros_like(run_ref)

    x = x_ref[...]
    for a_ref, w_ref in zip(a_refs, w_refs):
        x = x + jnp.dot(a_ref[...], w_ref[...], preferred_element_type=F32)
    xo_ref[...] = x
    ms = jnp.mean(x * x, axis=-1, keepdims=True)
    h = x * lax.rsqrt(ms + RMS_EPS) * g_ref[...]
    h_hi, h_lo = _split_hilo(h)
    h_ref[...] = h_hi
    wh = wh_ref[...]
    lg = (jnp.dot(h_hi, wh, preferred_element_type=F32) + jnp.dot(h_lo, wh, preferred_element_type=F32)
          + jnp.dot(h_hi, wl_ref[...], preferred_element_type=F32)) + b_ref[0:1]
    lane = lax.broadcasted_iota(jnp.int32, lg.shape, 1)
    neg_inf = -jnp.inf
    is_g = lane < N_GROUPS
    gl = jnp.where(is_g, lg, neg_inf)
    gmax = jnp.max(gl, axis=-1, keepdims=True)
    gidx = jnp.min(jnp.where(gl == gmax, lane, V7X_LANES), axis=-1, keepdims=True)
    g_top = 1.0 / jnp.sum(jnp.where(is_g, jnp.exp(gl - gmax), 0.0), axis=-1, keepdims=True)
    lo_e = N_GROUPS + EXPERTS_PER_GROUP * gidx
    el = jnp.where((lane >= lo_e) & (lane < lo_e + EXPERTS_PER_GROUP), lg, neg_inf)
    m1 = jnp.max(el, axis=-1, keepdims=True)
    i1 = jnp.min(jnp.where(el == m1, lane, V7X_LANES), axis=-1, keepdims=True)
    el2 = jnp.where(lane == i1, neg_inf, el)
    m2 = jnp.max(el2, axis=-1, keepdims=True)
    i2 = jnp.min(jnp.where(el2 == m2, lane, V7X_LANES), axis=-1, keepdims=True)
    t = jnp.exp(m2 - m1)
    gate1 = g_top / (1.0 + t)
    gate2 = g_top * t / (1.0 + t)
    oh1 = lane == i1
    oh2 = lane == i2
    oh = jnp.where(oh1 | oh2, 1.0, 0.0)
    before = jnp.dot(slt_ref[...], oh.astype(BF16), preferred_element_type=F32) + run_ref[0:1]
    rank1 = jnp.sum(jnp.where(oh1, before, 0.0), axis=-1, keepdims=True)
    rank2 = jnp.sum(jnp.where(oh2, before, 0.0), axis=-1, keepdims=True)
    run_ref[0:1] = run_ref[0:1] + jnp.sum(oh, axis=0, keepdims=True)
    cnt_ref[...] = jnp.broadcast_to(run_ref[0:1], cnt_ref.shape)
    vals = (i1 - N_GROUPS, i2 - N_GROUPS, rank1, rank2, gate1, gate2)
    out = jnp.zeros(lg.shape, F32)
    for li, val in enumerate(vals):
        out = jnp.where(lane == li, val.astype(F32), out)
    rt_ref[...] = out.T[0:V7X_SUBLANES]


def _proj_norm_route(x, acts, ws, g, w_router, b_router, tm):
    n, d = x.shape
    n_in = len(acts)
    nr = w_router.shape[1]
    wpad = jnp.zeros((d, V7X_LANES), F32).at[:, :nr].set(w_router)
    wh = wpad.astype(BF16)
    wl = (wpad - wh.astype(F32)).astype(BF16)
    bias = jnp.zeros((V7X_SUBLANES, V7X_LANES), F32).at[0, :nr].set(b_router)
    slt = jnp.asarray(np.tril(np.ones((tm, tm), np.float32), -1), BF16)
    const = lambda shape: pl.BlockSpec(shape, lambda i: (0, 0))
    rows = lambda width: pl.BlockSpec((tm, width), lambda i: (i, 0))
    return pl.pallas_call(
        functools.partial(_proj_norm_route_body, n_in=n_in),
        grid=(n // tm,),
        in_specs=([rows(d)] + [rows(a.shape[1]) for a in acts] + [const(w.shape) for w in ws]
                  + [const((1, d)), const((d, V7X_LANES)), const((d, V7X_LANES)),
                     const((V7X_SUBLANES, V7X_LANES)), const((tm, tm))]),
        out_specs=[rows(d), rows(d), pl.BlockSpec((V7X_SUBLANES, tm), lambda i: (0, i)),
                   const((V7X_SUBLANES, V7X_LANES))],
        out_shape=[jax.ShapeDtypeStruct((n, d), F32), jax.ShapeDtypeStruct((n, d), BF16),
                   jax.ShapeDtypeStruct((V7X_SUBLANES, n), F32),
                   jax.ShapeDtypeStruct((V7X_SUBLANES, V7X_LANES), F32)],
        scratch_shapes=[pltpu.VMEM((V7X_SUBLANES, V7X_LANES), F32)],
        compiler_params=_cparams(("arbitrary",)),
    )(x, *acts, *ws, g, wh, wl, bias, slt)


def _moe_body(te_ref, nv_ref, x_ref, wg_ref, wu_ref, wd_ref, o_ref, wgb_ref, wub_ref, wdb_ref):
    i = pl.program_id(0)
    valid = i < nv_ref[0]

    @pl.when(valid & ((i == 0) | (te_ref[i] != te_ref[jnp.maximum(i - 1, 0)])))
    def _():
        wgb_ref[...] = wg_ref[0].astype(BF16)
        wub_ref[...] = wu_ref[0].astype(BF16)
        wdb_ref[...] = wd_ref[0].astype(BF16)

    @pl.when(valid)
    def _():
        x = x_ref[...]
        gt = jnp.dot(x, wgb_ref[...], preferred_element_type=F32)
        up = jnp.dot(x, wub_ref[...], preferred_element_type=F32)
        hid = (gt * _sigmoid(gt)) * up
        o_ref[...] = jnp.dot(hid.astype(BF16), wdb_ref[...], preferred_element_type=F32).astype(o_ref.dtype)


def _moe_grouped(xs, tile_expert, n_valid, wg, wu, wd, layer):
    p_rows, d = xs.shape
    de = wg.shape[3]
    tm = MOE_TILE
    grid_spec = pltpu.PrefetchScalarGridSpec(
        num_scalar_prefetch=2,
        grid=(p_rows // tm,),
        in_specs=[pl.BlockSpec((tm, d), lambda i, te, nv: (i, 0)),
                  pl.BlockSpec((None, 1, d, de), lambda i, te, nv: (layer, te[i], 0, 0)),
                  pl.BlockSpec((None, 1, d, de), lambda i, te, nv: (layer, te[i], 0, 0)),
                  pl.BlockSpec((None, 1, de, d), lambda i, te, nv: (layer, te[i], 0, 0))],
        out_specs=pl.BlockSpec((tm, d), lambda i, te, nv: (i, 0)),
        scratch_shapes=[pltpu.VMEM((d, de), BF16), pltpu.VMEM((d, de), BF16), pltpu.VMEM((de, d), BF16)],
    )
    return pl.pallas_call(
        _moe_body,
        grid_spec=grid_spec,
        out_shape=jax.ShapeDtypeStruct((p_rows, d), BF16),
        compiler_params=_cparams(("arbitrary",)),
    )(tile_expert, n_valid, xs, wg, wu, wd)


def _mix_out_moe(x, acts, ws, g_norm, w_group, b_group, w_erouter, b_erouter, wg, wu, wd, layer, tm):
    n, d = x.shape
    x, hn, route, cnt = _proj_norm_route(x, acts, ws, g_norm, jnp.concatenate([w_group, w_erouter], axis=1),
                                         jnp.concatenate([b_group, b_erouter]), tm)
    expert_id = route[0:2].astype(jnp.int32)
    rank = route[2:4].astype(jnp.int32)
    gates = route[4:6]
    counts = cnt[0, N_GROUPS:N_GROUPS + N_EXPERTS].astype(jnp.int32)

    tmm = MOE_TILE
    padded = ((counts + tmm - 1) // tmm) * tmm
    ends = jnp.cumsum(padded)
    starts = ends - padded
    pos = starts[expert_id] + rank
    p_rows = 2 * n + N_EXPERTS * tmm
    n_tiles = p_rows // tmm
    tok = jnp.broadcast_to(jnp.arange(n, dtype=jnp.int32)[None, :], (2, n))
    row_token = jnp.zeros((p_rows,), jnp.int32).at[pos.reshape(-1)].set(
        tok.reshape(-1), mode="promise_in_bounds", unique_indices=True)
    tile_expert = jnp.minimum(
        jnp.searchsorted(ends, jnp.arange(n_tiles, dtype=jnp.int32) * tmm, side="right"), N_EXPERTS - 1
    ).astype(jnp.int32)
    n_valid = (ends[-1] // tmm).astype(jnp.int32).reshape(1)

    take_rows = lambda a, idx: jnp.take(a, idx, axis=0, mode="clip")
    xs = take_rows(hn, row_token)
    ys = _moe_grouped(xs, tile_expert, n_valid, wg, wu, wd, layer)
    y = (gates[0][:, None] * take_rows(ys, pos[0]).astype(F32)
         + gates[1][:, None] * take_rows(ys, pos[1]).astype(F32))
    return x + y


def _pad_rows(w, rows, offset=0):
    out = jnp.zeros((rows, w.shape[1]), w.dtype)
    return out.at[offset:offset + w.shape[0]].set(w)


def kernel(x, norm_mix, norm_ffn, w_in_even, mu_a, w0, decay_up, a0, iclr_up, gate_up, k_k, k_a, r_k, gn_w,
           gn_b, q_norm_g, k_norm_g, w_out_even, w_in_odd, conv_w, conv_b, w_rgate, b_rgate, w_igate,
           b_igate, lru_lambda, w_out_odd, w_group, b_group, w_erouter, b_erouter, exp_w_gate, exp_w_up,
           exp_w_down):
    bsz, t_len, d = x.shape
    n = bsz * t_len
    depth = norm_mix.shape[0]
    aw = decay_up.shape[2]
    d_lora, i_lora, g_lora = decay_up.shape[1], iclr_up.shape[1], gate_up.shape[1]
    bw = (w_in_even.shape[2] - (3 * aw + d_lora + i_lora + g_lora)) // 3
    lw = w_rgate.shape[1] * w_rgate.shape[2]
    assert d_lora + i_lora == V7X_LANES and g_lora <= 2 * V7X_LANES
    lora_w = 3 * V7X_LANES
    tm = 512
    xf = x.reshape(n, d)

    for layer in range(depth):
        gm = norm_mix[layer].reshape(1, d)
        if layer % 2 == 0:
            i = layer // 2
            w_in = w_in_even[i]
            o3 = 3 * aw
            a_cols = o3 + d_lora + i_lora + g_lora
            pad = jnp.zeros((d, lora_w - (d_lora + i_lora + g_lora)), F32)
            w_cat = jnp.concatenate([w_in[:, :a_cols], pad, w_in[:, a_cols:]], axis=1).astype(BF16)
            r, k, v, lo, qb, kb, vb = _norm_matmul(xf, gm, w_cat, (aw, aw, aw, lora_w, bw, bw, bw), tm)
            mu = mu_a[i]
            rows = [mu[:aw], mu[aw:2 * aw], mu[2 * aw:o3], w0[i], a0[i], k_k[i], k_a[i], r_k[i].reshape(-1),
                    gn_w[i], gn_b[i]]
            prm = jnp.zeros((16, aw), F32).at[:len(rows)].set(jnp.stack(rows))
            mul = jnp.zeros((V7X_SUBLANES, lora_w), F32).at[0, :a_cols - o3].set(mu[o3:])
            du = _pad_rows(decay_up[i], V7X_LANES, 0).astype(BF16)
            iu = _pad_rows(iclr_up[i], V7X_LANES, d_lora).astype(BF16)
            gu = _pad_rows(gate_up[i], 2 * V7X_LANES, 0).astype(BF16)
            ya = _rwkv(r, k, v, lo, prm, mul, du, iu, gu, bsz, t_len)
            qg = jnp.tile(q_norm_g[i], V7X_LANES // HEAD_DIM).reshape(1, V7X_LANES)
            kg = jnp.tile(k_norm_g[i], V7X_LANES // HEAD_DIM).reshape(1, V7X_LANES)
            yb = _stick_breaking(qb, kb, vb, qg, kg, bsz, t_len)
            w_out = w_out_even[i].astype(BF16)
            acts, ws = [ya, yb], [w_out[:aw], w_out[aw:]]
        else:
            j = layer // 2
            gate, rec = _norm_matmul(xf, gm, w_in_odd[j].astype(BF16), (lw, lw), tm)
            cw = jnp.zeros((V7X_SUBLANES, lw), F32).at[:conv_w.shape[1]].set(conv_w[j])
            pv = jnp.zeros((V7X_SUBLANES, lw), F32).at[:4].set(
                jnp.stack([conv_b[j], b_rgate[j], b_igate[j], lru_lambda[j]]))
            yl = _rg_lru(gate, rec, cw, pv, w_rgate[j].astype(BF16), w_igate[j].astype(BF16), bsz, t_len,
                         min(256, t_len))
            acts, ws = [yl], [w_out_odd[j].astype(BF16)]
        xf = _mix_out_moe(xf, acts, ws, norm_ffn[layer].reshape(1, d), w_group[layer], b_group[layer],
                          w_erouter[layer], b_erouter[layer], exp_w_gate, exp_w_up, exp_w_down, layer, tm)
    return xf.reshape(bsz, t_len, d)
```

```python
import functools
import math

import numpy as np
import jax
import jax.numpy as jnp
from jax import lax
from jax.experimental import pallas as pl
from jax.experimental.pallas import tpu as pltpu

F32 = jnp.float32
BF16 = jnp.bfloat16

V7X_LANES = 128
V7X_SUBLANES = 8
V7X_VMEM_LIMIT_BYTES = 48 * 1024 * 1024

HEAD_DIM = 64
RMS_EPS = 1e-6
GN_EPS = 64e-5
LRU_C = 8.0
N_GROUPS = 4
EXPERTS_PER_GROUP = 4
N_EXPERTS = 16
RWKV_CHUNK = 128
SB_BLOCK = 128
SB_KEY_GROUP = 4
SB_QUERY_GROUP = 2
MOE_TILE = 512
NEG_BIG = -1e30
LOG2_E = 1.4426950408889634


def _cparams(sem):
    return pltpu.CompilerParams(dimension_semantics=sem, vmem_limit_bytes=V7X_VMEM_LIMIT_BYTES)


def _mm(a, b):
    return jnp.dot(a.astype(BF16), b.astype(BF16), preferred_element_type=F32)


def _mm_nt(a, b):
    return lax.dot_general(a.astype(BF16), b.astype(BF16), (((1,), (1,)), ((), ())),
                           preferred_element_type=F32)


def _split_hilo(x):
    hi = x.astype(BF16)
    lo = (x - hi.astype(F32)).astype(BF16)
    return hi, lo


def _mm_hilo(x, m):
    hi, lo = _split_hilo(x)
    return (jnp.dot(hi, m, preferred_element_type=F32) + jnp.dot(lo, m, preferred_element_type=F32))


def _hilo_mm(m, x):
    hi, lo = _split_hilo(x)
    return (jnp.dot(m, hi, preferred_element_type=F32) + jnp.dot(m, lo, preferred_element_type=F32))


def _sigmoid(x):
    return 0.5 * jnp.tanh(0.5 * x) + 0.5


def _neg_abs(x):
    bits = pltpu.bitcast(x, jnp.uint32) | jnp.uint32(0x80000000)
    return pltpu.bitcast(bits, F32)


def _softplus(x):
    return jnp.maximum(x, 0.0) + jnp.log(1.0 + jnp.exp(-jnp.abs(x)))


def _norm_matmul_body(x_ref, g_ref, w_ref, *out_refs, col_splits):
    x = x_ref[...]
    ms = jnp.mean(x * x, axis=-1, keepdims=True)
    h = (x * lax.rsqrt(ms + RMS_EPS) * g_ref[...]).astype(BF16)
    off = 0
    for o_ref, c in zip(out_refs, col_splits):
        o_ref[...] = jnp.dot(h, w_ref[:, off:off + c], preferred_element_type=F32).astype(o_ref.dtype)
        off += c


def _norm_matmul(x, g, w, col_splits, tm):
    n, d = x.shape
    ctot = w.shape[1]
    assert sum(col_splits) == ctot and n % tm == 0
    return pl.pallas_call(
        functools.partial(_norm_matmul_body, col_splits=tuple(col_splits)),
        grid=(n // tm,),
        in_specs=[pl.BlockSpec((tm, d), lambda i: (i, 0)),
                  pl.BlockSpec((1, d), lambda i: (0, 0)),
                  pl.BlockSpec((d, ctot), lambda i: (0, 0))],
        out_specs=[pl.BlockSpec((tm, c), lambda i: (i, 0)) for c in col_splits],
        out_shape=[jax.ShapeDtypeStruct((n, c), BF16) for c in col_splits],
        compiler_params=_cparams(("parallel",)),
    )(x, g, w)


def _tri_inverse_many(mats, row, col):
    eye = (row == col).astype(F32)
    blk8 = (row // 8) == (col // 8)
    n1 = [jnp.where(blk8, a, 0.0) for a in mats]
    n2 = [_mm(n, n) for n in n1]
    ipn = [eye + n for n in n1]
    p = [i + _mm(i, m) for i, m in zip(ipn, n2)]
    n4 = [_mm(m, m) for m in n2]
    x = [q + _mm(q, m) for q, m in zip(p, n4)]
    k = 8
    c = mats[0].shape[0]
    while k < c:
        sel = ((row // (2 * k)) == (col // (2 * k))) & ((row // k) > (col // k))
        e = [jnp.where(sel, a, 0.0) for a in mats]
        xe = [_mm(xi, ei) for xi, ei in zip(x, e)]
        x = [xi + _mm(xei, xi) for xi, xei in zip(x, xe)]
        k *= 2
    return x


def _rwkv_body(r_ref, k_ref, v_ref, l_ref, p_ref, mul_ref, du_ref, iu_ref, gu_ref, bones_ref, lt_ref,
               o_ref, z_ref, carry_ref):
    c = RWKV_CHUNK
    aw = r_ref.shape[1]
    lw = l_ref.shape[1]
    n_pairs = aw // V7X_LANES

    @pl.when(pl.program_id(1) == 0)
    def _():
        z_ref[...] = jnp.zeros_like(z_ref)
        carry_ref[...] = jnp.zeros_like(carry_ref)

    row1 = lax.broadcasted_iota(jnp.int32, (c, 1), 0)

    def token_shift(raw, prev_last, mu):
        prev = jnp.where(row1 == 0, prev_last, pltpu.roll(raw, 1, 0))
        return raw + mu * (prev - raw)

    prm = p_ref[...]
    mu_r, mu_k, mu_v, w0, a0, k_k, k_a, r_k, gn_w, gn_b = (prm[i:i + 1] for i in range(10))

    r_raw = r_ref[...].astype(F32)
    k_raw = k_ref[...].astype(F32)
    v_raw = v_ref[...].astype(F32)
    l_raw = l_ref[...].astype(F32)
    r = token_shift(r_raw, carry_ref[0:1, 0:aw], mu_r)
    k = token_shift(k_raw, carry_ref[0:1, aw:2 * aw], mu_k)
    v = token_shift(v_raw, carry_ref[0:1, 2 * aw:3 * aw], mu_v)
    xl = token_shift(l_raw, carry_ref[0:1, 3 * aw:3 * aw + lw], mul_ref[0:1])
    carry_ref[0:1, 0:aw] = r_raw[c - 1:c]
    carry_ref[0:1, aw:2 * aw] = k_raw[c - 1:c]
    carry_ref[0:1, 2 * aw:3 * aw] = v_raw[c - 1:c]
    carry_ref[0:1, 3 * aw:3 * aw + lw] = l_raw[c - 1:c]

    l0 = xl[:, :V7X_LANES]
    l1 = xl[:, V7X_LANES:]
    dw = _mm(jnp.tanh(l0), du_ref[...])
    da = _mm(l0, iu_ref[...])
    g = _mm(_sigmoid(l1), gu_ref[...])
    w_log = -_softplus(-(w0 + dw)) - 0.5
    logw = -jnp.exp(w_log)
    a = _sigmoid(a0 + da)

    bones = bones_ref[...]
    kk = k * k_k
    ssq = _mm_hilo(kk * kk, bones)
    kk = kk * jnp.minimum(lax.rsqrt(ssq), 1e12)
    k_mod = k * (1.0 + (a - 1.0) * k_a)
    a_vec = -kk
    b_vec = kk * a

    cum = _hilo_mm(lt_ref[...], logw)
    cmid = cum[c // 2 - 1:c // 2]
    cend = cum[c - 1:c]
    emid = jnp.exp(cmid)
    a_s = a_vec * jnp.exp(cum - logw - cmid)
    a_t = a_s * emid
    r_s = r * jnp.exp(cum - cmid)
    r_t = r_s * emid
    e_neg = jnp.exp(cmid - cum)
    b_s = b_vec * e_neg
    k_s = k_mod * e_neg
    e_end = jnp.exp(cend - cum)
    b_p = b_vec * e_end
    k_p = k_mod * e_end
    p_end = jnp.exp(cend)

    row = lax.broadcasted_iota(jnp.int32, (c, c), 0)
    col = lax.broadcasted_iota(jnp.int32, (c, c), 1)
    strict = col < row
    incl = col <= row
    eye = row == col
    same_head = (row // HEAD_DIM) == (col // HEAD_DIM)
    m0 = lax.broadcasted_iota(jnp.int32, (1, V7X_LANES), 1) < HEAD_DIM

    def stack(x):
        return jnp.concatenate([jnp.where(m0, x, 0.0), jnp.where(m0, 0.0, x)], axis=0)

    pairs = range(n_pairs)
    sls = [slice(p * V7X_LANES, (p + 1) * V7X_LANES) for p in pairs]
    v_p = [v[:, sl] for sl in sls]
    gm = [_mm_nt(jnp.concatenate([stack(a_s[:, sl]), stack(r_s[:, sl])], axis=0),
                 jnp.concatenate([b_s[:, sl], k_s[:, sl]], axis=0)) for sl in sls]
    aab = [jnp.where(strict, gm[p][h * c:(h + 1) * c, 0:c], 0.0) for p in pairs for h in range(2)]
    tinv = _tri_inverse_many(aab, row, col)
    aak = [jnp.concatenate([jnp.where(strict, gm[p][h * c:(h + 1) * c, c:2 * c], 0.0) for h in range(2)],
                           axis=1) for p in pairs]
    arb = [jnp.concatenate([jnp.where(incl, gm[p][(2 + h) * c:(3 + h) * c, 0:c], 0.0) for h in range(2)],
                           axis=1) for p in pairs]
    ark = [jnp.concatenate([jnp.where(incl, gm[p][(2 + h) * c:(3 + h) * c, c:2 * c], 0.0) for h in range(2)],
                           axis=1) for p in pairs]
    vs = [stack(x) for x in v_p]
    akv = [_mm(aak[p], vs[p]) for p in pairs]
    wu = [_mm(jnp.concatenate([tinv[2 * p], tinv[2 * p + 1]], axis=1),
              jnp.concatenate([stack(a_t[:, sls[p]]), stack(akv[p])], axis=1)) for p in pairs]
    w_t = [x[:, :V7X_LANES] for x in wu]
    u_v = [x[:, V7X_LANES:] for x in wu]
    q_t = [r_t[:, sls[p]] + _mm(arb[p], stack(w_t[p])) for p in pairs]
    y_v = [_mm(jnp.concatenate([arb[p], ark[p]], axis=1),
               jnp.concatenate([stack(u_v[p]), vs[p]], axis=0)) for p in pairs]
    bp_t = [b_p[:, sl].T for sl in sls]
    kp_t = [k_p[:, sl].T for sl in sls]
    m_mat = [jnp.where(same_head, _mm(bp_t[p], w_t[p]), 0.0) for p in pairs]
    n_mat = [jnp.where(same_head, _mm(jnp.concatenate([bp_t[p], kp_t[p]], axis=1),
                                      jnp.concatenate([u_v[p], v_p[p]], axis=0)), 0.0) for p in pairs]
    z_old = [z_ref[p] for p in pairs]
    y_parts = [_mm(q_t[p], z_old[p]) + y_v[p] for p in pairs]
    for p in pairs:
        p_col = jnp.sum(jnp.where(eye, p_end[:, sls[p]], 0.0), axis=1, keepdims=True)
        z_ref[p] = p_col * z_old[p] + _mm(m_mat[p], z_old[p]) + n_mat[p]

    y = jnp.concatenate(y_parts, axis=1)
    inv_hd = 1.0 / HEAD_DIM
    mean = _mm_hilo(y, bones) * inv_hd
    d = y - mean
    var = _mm_hilo(d * d, bones) * inv_hd
    yn = d * lax.rsqrt(var + GN_EPS) * gn_w + gn_b
    bonus = _mm_hilo(r * k_mod * r_k, bones) * v
    o_ref[...] = ((yn + bonus) * g).astype(o_ref.dtype)


def _rwkv(r, k, v, l, prm, mul, du, iu, gu, bsz, t_len):
    n, aw = r.shape
    lw = l.shape[1]
    c = RWKV_CHUNK
    nt = t_len // c
    heads = np.arange(aw) // HEAD_DIM
    bones = jnp.asarray((heads[:, None] == heads[None, :]).astype(np.float32), BF16)
    lt = jnp.asarray(np.tril(np.ones((c, c), np.float32)), BF16)
    tok = lambda w: pl.BlockSpec((c, w), lambda b, t: (b * nt + t, 0))
    full = lambda arr: pl.BlockSpec(arr.shape, lambda b, t: (0,) * arr.ndim)
    return pl.pallas_call(
        _rwkv_body,
        grid=(bsz, nt),
        in_specs=[tok(aw), tok(aw), tok(aw), tok(lw), full(prm), full(mul), full(du), full(iu), full(gu),
                  full(bones), full(lt)],
        out_specs=tok(aw),
        out_shape=jax.ShapeDtypeStruct((n, aw), BF16),
        scratch_shapes=[pltpu.VMEM((aw // V7X_LANES, V7X_LANES, V7X_LANES), F32),
                        pltpu.VMEM((V7X_SUBLANES, 3 * aw + lw), F32)],
        compiler_params=_cparams(("parallel", "arbitrary")),
    )(r, k, v, l, prm, mul, du, iu, gu, bones, lt)


def _sb_body(q_ref, k_ref, v_ref, qg_ref, kg_ref, bo_ref, su_ref, o_ref, qm_ref, kn_ref, acc_ref, car_ref):
    t_len = q_ref.shape[0]
    blk = SB_BLOCK
    m0 = lax.broadcasted_iota(jnp.int32, (1, V7X_LANES), 1) < HEAD_DIM
    scale = LOG2_E / math.sqrt(HEAD_DIM)
    inv_hd = 1.0 / HEAD_DIM

    def prep(i, _):
        s0 = pl.multiple_of(i * blk, blk)
        q = q_ref[pl.ds(s0, blk), :].astype(F32)
        qn = q * lax.rsqrt(_mm_hilo(q * q, bo_ref[...]) * inv_hd + RMS_EPS) * (qg_ref[...] * scale)
        qm_ref[0, pl.ds(s0, blk), :] = jnp.where(m0, qn, 0.0).astype(BF16)
        qm_ref[1, pl.ds(s0, blk), :] = jnp.where(m0, 0.0, qn).astype(BF16)
        kx = k_ref[pl.ds(s0, blk), :].astype(F32)
        kn = kx * lax.rsqrt(_mm_hilo(kx * kx, bo_ref[...]) * inv_hd + RMS_EPS) * kg_ref[...]
        kn_ref[pl.ds(s0, blk), :] = kn.astype(BF16)
        return 0

    lax.fori_loop(0, t_len // blk, prep, 0)

    n_qblocks = t_len // blk
    chains = [(qq, h) for qq in range(SB_QUERY_GROUP) for h in range(2)]

    def process(q0, k0, n_kb, diag):
        width = n_kb * blk
        kb = kn_ref[pl.ds(k0, width), :]
        vb = v_ref[pl.ds(k0, width), :]
        z = [_mm_nt(qm_ref[h, pl.ds(q0 + qq * blk, blk), :], kb) for qq, h in chains]
        sp = [jnp.maximum(zc, 0.0) + jnp.log2(1.0 + jnp.exp2(_neg_abs(zc))) for zc in z]
        if diag:
            row = lax.broadcasted_iota(jnp.int32, (blk, width), 0)
            col = lax.broadcasted_iota(jnp.int32, (blk, width), 1)
            causal = [(col + k0) < (row + q0 + qq * blk) for qq in range(SB_QUERY_GROUP)]
            sp = [jnp.where(causal[qq], x, 0.0) for (qq, _), x in zip(chains, sp)]
        sp_lp = [x.astype(BF16) for x in sp]
        su = su_ref[...]
        rs = [[jnp.dot(x[:, b * blk:(b + 1) * blk], su, preferred_element_type=F32) for b in range(n_kb)]
              for x in sp_lp]
        att = []
        for c, (qq, _) in enumerate(chains):
            car = car_ref[c]
            after = [None] * n_kb
            for b in reversed(range(n_kb)):
                after[b] = rs[c][b][:, :blk] + car
                car = car + rs[c][b][:, blk:]
            car_ref[c] = car
            logit = z[c] - sp[c] - jnp.concatenate(after, axis=1)
            if diag:
                logit = jnp.where(causal[qq], logit, NEG_BIG)
            att.append(jnp.exp2(logit))
        for c in range(len(chains)):
            acc_ref[c] += _mm(att[c], vb)

    def sweep_pair(q0, n_wide, half_step):
        acc_ref[...] = jnp.zeros_like(acc_ref)
        car_ref[...] = jnp.zeros_like(car_ref)
        process(q0, q0, SB_QUERY_GROUP, True)
        base = q0
        if half_step:
            base = q0 - SB_QUERY_GROUP * blk
            process(q0, base, SB_QUERY_GROUP, False)
        wide = SB_KEY_GROUP * blk

        def kstep(jj, _):
            process(q0, pl.multiple_of(base - (jj + 1) * wide, wide), SB_KEY_GROUP, False)
            return 0

        lax.fori_loop(0, n_wide, kstep, 0)
        for qq in range(SB_QUERY_GROUP):
            o_ref[pl.ds(q0 + qq * blk, blk), :] = jnp.where(m0, acc_ref[2 * qq], acc_ref[2 * qq + 1]
                                                            ).astype(o_ref.dtype)

    quad = SB_KEY_GROUP * blk
    n_quads = n_qblocks // SB_KEY_GROUP

    def quad_step(m, _):
        q0 = pl.multiple_of(m * quad, quad)
        sweep_pair(q0, m, False)
        sweep_pair(q0 + SB_QUERY_GROUP * blk, m, True)
        return 0

    lax.fori_loop(0, n_quads, quad_step, 0)
    if n_qblocks % SB_KEY_GROUP:
        sweep_pair(n_quads * quad, n_quads, False)


def _stick_breaking(q, k, v, qg, kg, bsz, t_len):
    n, bw = q.shape
    n_pairs = bw // V7X_LANES
    blk = SB_BLOCK
    heads = np.arange(V7X_LANES) // HEAD_DIM
    bo = jnp.asarray((heads[:, None] == heads[None, :]).astype(np.float32), BF16)
    idx = np.arange(blk)
    su = np.concatenate([(idx[:, None] > idx[None, :]).astype(np.float32), np.ones((blk, blk), np.float32)], axis=1)
    su = jnp.asarray(su, BF16)
    tok = pl.BlockSpec((t_len, V7X_LANES), lambda b, p: (b, p))
    full = lambda arr: pl.BlockSpec(arr.shape, lambda b, p: (0,) * arr.ndim)
    assert SB_KEY_GROUP == 2 * SB_QUERY_GROUP and t_len % (SB_QUERY_GROUP * blk) == 0
    qgroup = SB_QUERY_GROUP
    return pl.pallas_call(
        _sb_body,
        grid=(bsz, n_pairs),
        in_specs=[tok, tok, tok, full(qg), full(kg), full(bo), full(su)],
        out_specs=tok,
        out_shape=jax.ShapeDtypeStruct((n, bw), BF16),
        scratch_shapes=[pltpu.VMEM((2, t_len, V7X_LANES), BF16),
                        pltpu.VMEM((t_len, V7X_LANES), BF16),
                        pltpu.VMEM((2 * qgroup, blk, V7X_LANES), F32),
                        pltpu.VMEM((2 * qgroup, blk, blk), F32)],
        compiler_params=_cparams(("parallel", "parallel")),
    )(q, k, v, qg, kg, bo, su)


def _lru_body(gate_ref, rec_ref, cw_ref, pv_ref, wr_ref, wi_ref, o_ref, prev_ref, h_ref):
    tt = rec_ref.shape[0]
    width = rec_ref.shape[1]
    n_blocks = width // V7X_LANES

    @pl.when(pl.program_id(1) == 0)
    def _():
        prev_ref[...] = jnp.zeros_like(prev_ref)
        h_ref[...] = jnp.zeros_like(h_ref)

    row8 = lax.broadcasted_iota(jnp.int32, (V7X_SUBLANES, 1), 0)

    for gi in range(n_blocks):
        sl = slice(gi * V7X_LANES, (gi + 1) * V7X_LANES)
        u = rec_ref[:, sl].astype(F32)
        prev = prev_ref[:, sl]

        def shifted(s):
            ru = pltpu.roll(u, s, 0)
            first = jnp.where(row8 < s, pltpu.roll(prev, s, 0), ru[0:V7X_SUBLANES])
            return jnp.concatenate([first, ru[V7X_SUBLANES:]], axis=0)

        cw = cw_ref[:, sl]
        rc = (cw[3:4] * u + cw[2:3] * shifted(1) + cw[1:2] * shifted(2) + cw[0:1] * shifted(3)
              + pv_ref[0:1, sl])
        prev_ref[:, sl] = u[tt - V7X_SUBLANES:tt]

        rgate = _sigmoid(_mm(rc, wr_ref[gi]) + pv_ref[1:2, sl])
        igate = _sigmoid(_mm(rc, wi_ref[gi]) + pv_ref[2:3, sl])
        log_a = -LRU_C * rgate * _softplus(-pv_ref[3:4, sl])
        a = jnp.exp(log_a)
        om = 1.0 - a * a
        b = jnp.where(om > 0.0, om * lax.rsqrt(om), 0.0) * (igate * rc)

        s = 1
        while s < V7X_SUBLANES:
            ra = pltpu.roll(a, s, 0)
            rb = pltpu.roll(b, s, 0)
            a_sh = jnp.concatenate([jnp.where(row8 >= s, ra[0:V7X_SUBLANES], 1.0), ra[V7X_SUBLANES:]], axis=0)
            b_sh = jnp.concatenate([jnp.where(row8 >= s, rb[0:V7X_SUBLANES], 0.0), rb[V7X_SUBLANES:]], axis=0)
            b = a * b_sh + b
            a = a * a_sh
            s *= 2
        n_groups = tt // V7X_SUBLANES
        ag = [a[j * V7X_SUBLANES:(j + 1) * V7X_SUBLANES] for j in range(n_groups)]
        bg = [b[j * V7X_SUBLANES:(j + 1) * V7X_SUBLANES] for j in range(n_groups)]
        k = 1
        while k < n_groups:
            bg = [bg[j] if j < k else ag[j] * bg[j - k] + bg[j] for j in range(n_groups)]
            ag = [ag[j] if j < k else ag[j] * ag[j - k] for j in range(n_groups)]
            k *= 2
        h_in = h_ref[0:1, sl]
        h = jnp.concatenate([ag[j] * h_in + bg[j] for j in range(n_groups)], axis=0)
        h_ref[0:1, sl] = h[tt - 1:tt]
        o_ref[:, sl] = (jax.nn.gelu(gate_ref[:, sl].astype(F32)) * h).astype(o_ref.dtype)


def _rg_lru(gate, rec, cw, pv, wr, wi, bsz, t_len, tt):
    n, width = rec.shape
    nt = t_len // tt
    tok = pl.BlockSpec((tt, width), lambda b, t: (b * nt + t, 0))
    full = lambda arr: pl.BlockSpec(arr.shape, lambda b, t: (0,) * arr.ndim)
    return pl.pallas_call(
        _lru_body,
        grid=(bsz, nt),
        in_specs=[tok, tok, full(cw), full(pv), full(wr), full(wi)],
        out_specs=tok,
        out_shape=jax.ShapeDtypeStruct((n, width), BF16),
        scratch_shapes=[pltpu.VMEM((V7X_SUBLANES, width), F32), pltpu.VMEM((V7X_SUBLANES, width), F32)],
        compiler_params=_cparams(("parallel", "arbitrary")),
    )(gate, rec, cw, pv, wr, wi)


def _proj_norm_route_body(*refs, n_in):
    x_ref = refs[0]
    a_refs = refs[1:1 + n_in]
    w_refs = refs[1 + n_in:1 + 2 * n_in]
    g_ref, wh_ref, wl_ref, b_ref, slt_ref, xo_ref, h_ref, rt_ref, cnt_ref, run_ref = refs[1 + 2 * n_in:]

    @pl.when(pl.program_id(0) == 0)
    def _():
        run_ref[...] = jnp.zeros_like(run_ref)

    x = x_ref[...]
    for a_ref, w_ref in zip(a_refs, w_refs):
        x = x + jnp.dot(a_ref[...], w_ref[...], preferred_element_type=F32)
    xo_ref[...] = x
    ms = jnp.mean(x * x, axis=-1, keepdims=True)
    h = x * lax.rsqrt(ms + RMS_EPS) * g_ref[...]
    h_hi, h_lo = _split_hilo(h)
    h_ref[...] = h_hi
    wh = wh_ref[...]
    lg = (jnp.dot(h_hi, wh, preferred_element_type=F32) + jnp.dot(h_lo, wh, preferred_element_type=F32)
          + jnp.dot(h_hi, wl_ref[...], preferred_element_type=F32)) + b_ref[0:1]
    lane = lax.broadcasted_iota(jnp.int32, lg.shape, 1)
    neg_inf = -jnp.inf
    is_g = lane < N_GROUPS
    gl = jnp.where(is_g, lg, neg_inf)
    gmax = jnp.max(gl, axis=-1, keepdims=True)
    gidx = jnp.min(jnp.where(gl == gmax, lane, V7X_LANES), axis=-1, keepdims=True)
    g_top = 1.0 / jnp.sum(jnp.where(is_g, jnp.exp(gl - gmax), 0.0), axis=-1, keepdims=True)
    lo_e = N_GROUPS + EXPERTS_PER_GROUP * gidx
    el = jnp.where((lane >= lo_e) & (lane < lo_e + EXPERTS_PER_GROUP), lg, neg_inf)
    m1 = jnp.max(el, axis=-1, keepdims=True)
    i1 = jnp.min(jnp.where(el == m1, lane, V7X_LANES), axis=-1, keepdims=True)
    el2 = jnp.where(lane == i1, neg_inf, el)
    m2 = jnp.max(el2, axis=-1, keepdims=True)
    i2 = jnp.min(jnp.where(el2 == m2, lane, V7X_LANES), axis=-1, keepdims=True)
    t = jnp.exp(m2 - m1)
    gate1 = g_top / (1.0 + t)
    gate2 = g_top * t / (1.0 + t)
    oh1 = lane == i1
    oh2 = lane == i2
    oh = jnp.where(oh1 | oh2, 1.0, 0.0)
    before = jnp.dot(slt_ref[...], oh.astype(BF16), preferred_element_type=F32) + run_ref[0:1]
    rank1 = jnp.sum(jnp.where(oh1, before, 0.0), axis=-1, keepdims=True)
    rank2 = jnp.sum(jnp.where(oh2, before, 0.0), axis=-1, keepdims=True)
    run_ref[0:1] = run_ref[0:1] + jnp.sum(oh, axis=0, keepdims=True)
    cnt_ref[...] = jnp.broadcast_to(run_ref[0:1], cnt_ref.shape)
    vals = (i1 - N_GROUPS, i2 - N_GROUPS, rank1, rank2, gate1, gate2)
    out = jnp.zeros(lg.shape, F32)
    for li, val in enumerate(vals):
        out = jnp.where(lane == li, val.astype(F32), out)
    rt_ref[...] = out.T[0:V7X_SUBLANES]


def _proj_norm_route(x, acts, ws, g, w_router, b_router, tm):
    n, d = x.shape
    n_in = len(acts)
    nr = w_router.shape[1]
    wpad = jnp.zeros((d, V7X_LANES), F32).at[:, :nr].set(w_router)
    wh = wpad.astype(BF16)
    wl = (wpad - wh.astype(F32)).astype(BF16)
    bias = jnp.zeros((V7X_SUBLANES, V7X_LANES), F32).at[0, :nr].set(b_router)
    slt = jnp.asarray(np.tril(np.ones((tm, tm), np.float32), -1), BF16)
    const = lambda shape: pl.BlockSpec(shape, lambda i: (0, 0))
    rows = lambda width: pl.BlockSpec((tm, width), lambda i: (i, 0))
    return pl.pallas_call(
        functools.partial(_proj_norm_route_body, n_in=n_in),
        grid=(n // tm,),
        in_specs=([rows(d)] + [rows(a.shape[1]) for a in acts] + [const(w.shape) for w in ws]
                  + [const((1, d)), const((d, V7X_LANES)), const((d, V7X_LANES)),
                     const((V7X_SUBLANES, V7X_LANES)), const((tm, tm))]),
        out_specs=[rows(d), rows(d), pl.BlockSpec((V7X_SUBLANES, tm), lambda i: (0, i)),
                   const((V7X_SUBLANES, V7X_LANES))],
        out_shape=[jax.ShapeDtypeStruct((n, d), F32), jax.ShapeDtypeStruct((n, d), BF16),
                   jax.ShapeDtypeStruct((V7X_SUBLANES, n), F32),
                   jax.ShapeDtypeStruct((V7X_SUBLANES, V7X_LANES), F32)],
        scratch_shapes=[pltpu.VMEM((V7X_SUBLANES, V7X_LANES), F32)],
        compiler_params=_cparams(("arbitrary",)),
    )(x, *acts, *ws, g, wh, wl, bias, slt)


def _moe_body(te_ref, nv_ref, x_ref, wg_ref, wu_ref, wd_ref, o_ref, wgb_ref, wub_ref, wdb_ref):
    i = pl.program_id(0)
    valid = i < nv_ref[0]

    @pl.when(valid & ((i == 0) | (te_ref[i] != te_ref[jnp.maximum(i - 1, 0)])))
    def _():
        wgb_ref[...] = wg_ref[0].astype(BF16)
        wub_ref[...] = wu_ref[0].astype(BF16)
        wdb_ref[...] = wd_ref[0].astype(BF16)

    @pl.when(valid)
    def _():
        x = x_ref[...]
        gt = jnp.dot(x, wgb_ref[...], preferred_element_type=F32)
        up = jnp.dot(x, wub_ref[...], preferred_element_type=F32)
        hid = (gt * _sigmoid(gt)) * up
        o_ref[...] = jnp.dot(hid.astype(BF16), wdb_ref[...], preferred_element_type=F32).astype(o_ref.dtype)


def _moe_grouped(xs, tile_expert, n_valid, wg, wu, wd, layer):
    p_rows, d = xs.shape
    de = wg.shape[3]
    tm = MOE_TILE
    grid_spec = pltpu.PrefetchScalarGridSpec(
        num_scalar_prefetch=2,
        grid=(p_rows // tm,),
        in_specs=[pl.BlockSpec((tm, d), lambda i, te, nv: (i, 0)),
                  pl.BlockSpec((None, 1, d, de), lambda i, te, nv: (layer, te[i], 0, 0)),
                  pl.BlockSpec((None, 1, d, de), lambda i, te, nv: (layer, te[i], 0, 0)),
                  pl.BlockSpec((None, 1, de, d), lambda i, te, nv: (layer, te[i], 0, 0))],
        out_specs=pl.BlockSpec((tm, d), lambda i, te, nv: (i, 0)),
        scratch_shapes=[pltpu.VMEM((d, de), BF16), pltpu.VMEM((d, de), BF16), pltpu.VMEM((de, d), BF16)],
    )
    return pl.pallas_call(
        _moe_body,
        grid_spec=grid_spec,
        out_shape=jax.ShapeDtypeStruct((p_rows, d), BF16),
        compiler_params=_cparams(("arbitrary",)),
    )(tile_expert, n_valid, xs, wg, wu, wd)


def _mix_out_moe(x, acts, ws, g_norm, w_group, b_group, w_erouter, b_erouter, wg, wu, wd, layer, tm):
    n, d = x.shape
    x, hn, route, cnt = _proj_norm_route(x, acts, ws, g_norm, jnp.concatenate([w_group, w_erouter], axis=1),
                                         jnp.concatenate([b_group, b_erouter]), tm)
    expert_id = route[0:2].astype(jnp.int32)
    rank = route[2:4].astype(jnp.int32)
    gates = route[4:6]
    counts = cnt[0, N_GROUPS:N_GROUPS + N_EXPERTS].astype(jnp.int32)

    tmm = MOE_TILE
    padded = ((counts + tmm - 1) // tmm) * tmm
    ends = jnp.cumsum(padded)
    starts = ends - padded
    start_of = jnp.zeros_like(expert_id)
    for e in range(N_EXPERTS):
        start_of = jnp.where(expert_id == e, starts[e], start_of)
    pos = start_of + rank
    p_rows = 2 * n + N_EXPERTS * tmm
    n_tiles = p_rows // tmm
    tok = jnp.broadcast_to(jnp.arange(n, dtype=jnp.int32)[None, :], (2, n))
    row_token = (jnp.arange(p_rows, dtype=jnp.int32) % n).at[pos.reshape(-1)].set(
        tok.reshape(-1), mode="promise_in_bounds", unique_indices=True)
    tile_expert = jnp.minimum(
        jnp.searchsorted(ends, jnp.arange(n_tiles, dtype=jnp.int32) * tmm, side="right"), N_EXPERTS - 1
    ).astype(jnp.int32)
    n_valid = (ends[-1] // tmm).astype(jnp.int32).reshape(1)

    take_rows = lambda a, idx: jnp.take(a, idx, axis=0, mode="clip")
    xs = take_rows(hn, row_token)
    ys = _moe_grouped(xs, tile_expert, n_valid, wg, wu, wd, layer)
    y = (gates[0][:, None] * take_rows(ys, pos[0]).astype(F32)
         + gates[1][:, None] * take_rows(ys, pos[1]).astype(F32))
    return x + y


def _pad_rows(w, rows, offset=0):
    out = jnp.zeros((rows, w.shape[1]), w.dtype)
    return out.at[offset:offset + w.shape[0]].set(w)


def kernel(x, norm_mix, norm_ffn, w_in_even, mu_a, w0, decay_up, a0, iclr_up, gate_up, k_k, k_a, r_k, gn_w,
           gn_b, q_norm_g, k_norm_g, w_out_even, w_in_odd, conv_w, conv_b, w_rgate, b_rgate, w_igate,
           b_igate, lru_lambda, w_out_odd, w_group, b_group, w_erouter, b_erouter, exp_w_gate, exp_w_up,
           exp_w_down):
    bsz, t_len, d = x.shape
    n = bsz * t_len
    depth = norm_mix.shape[0]
    aw = decay_up.shape[2]
    d_lora, i_lora, g_lora = decay_up.shape[1], iclr_up.shape[1], gate_up.shape[1]
    bw = (w_in_even.shape[2] - (3 * aw + d_lora + i_lora + g_lora)) // 3
    lw = w_rgate.shape[1] * w_rgate.shape[2]
    assert d_lora + i_lora == V7X_LANES and g_lora <= 2 * V7X_LANES
    lora_w = 3 * V7X_LANES
    tm = 512
    xf = x.reshape(n, d)

    for layer in range(depth):
        gm = norm_mix[layer].reshape(1, d)
        if layer % 2 == 0:
            i = layer // 2
            w_in = w_in_even[i]
            o3 = 3 * aw
            a_cols = o3 + d_lora + i_lora + g_lora
            pad = jnp.zeros((d, lora_w - (d_lora + i_lora + g_lora)), F32)
            w_cat = jnp.concatenate([w_in[:, :a_cols], pad, w_in[:, a_cols:]], axis=1).astype(BF16)
            r, k, v, lo, qb, kb, vb = _norm_matmul(xf, gm, w_cat, (aw, aw, aw, lora_w, bw, bw, bw), tm)
            mu = mu_a[i]
            rows = [mu[:aw], mu[aw:2 * aw], mu[2 * aw:o3], w0[i], a0[i], k_k[i], k_a[i], r_k[i].reshape(-1),
                    gn_w[i], gn_b[i]]
            prm = jnp.zeros((16, aw), F32).at[:len(rows)].set(jnp.stack(rows))
            mul = jnp.zeros((V7X_SUBLANES, lora_w), F32).at[0, :a_cols - o3].set(mu[o3:])
            du = _pad_rows(decay_up[i], V7X_LANES, 0).astype(BF16)
            iu = _pad_rows(iclr_up[i], V7X_LANES, d_lora).astype(BF16)
            gu = _pad_rows(gate_up[i], 2 * V7X_LANES, 0).astype(BF16)
            ya = _rwkv(r, k, v, lo, prm, mul, du, iu, gu, bsz, t_len)
            qg = jnp.tile(q_norm_g[i], V7X_LANES // HEAD_DIM).reshape(1, V7X_LANES)
            kg = jnp.tile(k_norm_g[i], V7X_LANES // HEAD_DIM).reshape(1, V7X_LANES)
            yb = _stick_breaking(qb, kb, vb, qg, kg, bsz, t_len)
            w_out = w_out_even[i].astype(BF16)
            acts, ws = [ya, yb], [w_out[:aw], w_out[aw:]]
        else:
            j = layer // 2
            gate, rec = _norm_matmul(xf, gm, w_in_odd[j].astype(BF16), (lw, lw), tm)
            cw = jnp.zeros((V7X_SUBLANES, lw), F32).at[:conv_w.shape[1]].set(conv_w[j])
            pv = jnp.zeros((V7X_SUBLANES, lw), F32).at[:4].set(
                jnp.stack([conv_b[j], b_rgate[j], b_igate[j], lru_lambda[j]]))
            yl = _rg_lru(gate, rec, cw, pv, w_rgate[j].astype(BF16), w_igate[j].astype(BF16), bsz, t_len,
                         min(256, t_len))
            acts, ws = [yl], [w_out_odd[j].astype(BF16)]
        xf = _mix_out_moe(xf, acts, ws, norm_ffn[layer].reshape(1, d), w_group[layer], b_group[layer],
                          w_erouter[layer], b_erouter[layer], exp_w_gate, exp_w_up, exp_w_down, layer, tm)
    return xf.reshape(bsz, t_len, d)
```

```python
import functools
import math

import numpy as np
import jax
import jax.numpy as jnp
from jax import lax
from jax.experimental import pallas as pl
from jax.experimental.pallas import tpu as pltpu

F32 = jnp.float32
BF16 = jnp.bfloat16

V7X_LANES = 128
V7X_SUBLANES = 8
V7X_VMEM_LIMIT_BYTES = 48 * 1024 * 1024

HEAD_DIM = 64
RMS_EPS = 1e-6
GN_EPS = 64e-5
LRU_C = 8.0
N_GROUPS = 4
EXPERTS_PER_GROUP = 4
N_EXPERTS = 16
RWKV_CHUNK = 128
RWKV_CHUNKS_PER_STEP = 2
SB_BLOCK = 128
SB_KEY_GROUP = 4
SB_QUERY_GROUP = 2
MOE_TILE = 512
NEG_BIG = -1e30
LOG2_E = 1.4426950408889634


def _cparams(sem):
    return pltpu.CompilerParams(dimension_semantics=sem, vmem_limit_bytes=V7X_VMEM_LIMIT_BYTES)


def _mm(a, b):
    return jnp.dot(a.astype(BF16), b.astype(BF16), preferred_element_type=F32)


def _mm_nt(a, b):
    return lax.dot_general(a.astype(BF16), b.astype(BF16), (((1,), (1,)), ((), ())),
                           preferred_element_type=F32)


def _split_hilo(x):
    hi = x.astype(BF16)
    lo = (x - hi.astype(F32)).astype(BF16)
    return hi, lo


def _mm_hilo(x, m):
    hi, lo = _split_hilo(x)
    return (jnp.dot(hi, m, preferred_element_type=F32) + jnp.dot(lo, m, preferred_element_type=F32))


def _hilo_mm(m, x):
    hi, lo = _split_hilo(x)
    return (jnp.dot(m, hi, preferred_element_type=F32) + jnp.dot(m, lo, preferred_element_type=F32))


def _sigmoid(x):
    return 0.5 * jnp.tanh(0.5 * x) + 0.5


def _neg_abs(x):
    bits = pltpu.bitcast(x, jnp.uint32) | jnp.uint32(0x80000000)
    return pltpu.bitcast(bits, F32)


def _softplus(x):
    return jnp.maximum(x, 0.0) + jnp.log(1.0 + jnp.exp(-jnp.abs(x)))


def _norm_matmul_body(x_ref, g_ref, w_ref, *out_refs, col_splits):
    x = x_ref[...]
    ms = jnp.mean(x * x, axis=-1, keepdims=True)
    h = (x * lax.rsqrt(ms + RMS_EPS) * g_ref[...]).astype(BF16)
    off = 0
    for o_ref, c in zip(out_refs, col_splits):
        o_ref[...] = jnp.dot(h, w_ref[:, off:off + c], preferred_element_type=F32).astype(o_ref.dtype)
        off += c


def _norm_matmul(x, g, w, col_splits, tm):
    n, d = x.shape
    ctot = w.shape[1]
    assert sum(col_splits) == ctot and n % tm == 0
    return pl.pallas_call(
        functools.partial(_norm_matmul_body, col_splits=tuple(col_splits)),
        grid=(n // tm,),
        in_specs=[pl.BlockSpec((tm, d), lambda i: (i, 0)),
                  pl.BlockSpec((1, d), lambda i: (0, 0)),
                  pl.BlockSpec((d, ctot), lambda i: (0, 0))],
        out_specs=[pl.BlockSpec((tm, c), lambda i: (i, 0)) for c in col_splits],
        out_shape=[jax.ShapeDtypeStruct((n, c), BF16) for c in col_splits],
        compiler_params=_cparams(("parallel",)),
    )(x, g, w)


def _tri_inverse_many(mats, row, col):
    eye = (row == col).astype(F32)
    blk8 = (row // 8) == (col // 8)
    n1 = [jnp.where(blk8, a, 0.0) for a in mats]
    n2 = [_mm(n, n) for n in n1]
    ipn = [eye + n for n in n1]
    p = [i + _mm(i, m) for i, m in zip(ipn, n2)]
    n4 = [_mm(m, m) for m in n2]
    x = [q + _mm(q, m) for q, m in zip(p, n4)]
    k = 8
    c = mats[0].shape[0]
    while k < c:
        sel = ((row // (2 * k)) == (col // (2 * k))) & ((row // k) > (col // k))
        e = [jnp.where(sel, a, 0.0) for a in mats]
        xe = [_mm(xi, ei) for xi, ei in zip(x, e)]
        x = [xi + _mm(xei, xi) for xi, xei in zip(x, xe)]
        k *= 2
    return x


def _rwkv_body(r_ref, k_ref, v_ref, l_ref, p_ref, mul_ref, du_ref, iu_ref, gu_ref, bones_ref, lt_ref,
               o_ref, z_ref, carry_ref):
    c = RWKV_CHUNK
    rows = r_ref.shape[0]
    n_chunks = rows // c
    aw = r_ref.shape[1]
    lw = l_ref.shape[1]
    n_pairs = aw // V7X_LANES

    @pl.when(pl.program_id(1) == 0)
    def _():
        z_ref[...] = jnp.zeros_like(z_ref)
        carry_ref[...] = jnp.zeros_like(carry_ref)

    row1 = lax.broadcasted_iota(jnp.int32, (rows, 1), 0)

    def token_shift(raw, prev_last, mu):
        prev = jnp.where(row1 == 0, prev_last, pltpu.roll(raw, 1, 0))
        return raw + mu * (prev - raw)

    prm = p_ref[...]
    mu_r, mu_k, mu_v, w0, a0, k_k, k_a, r_k, gn_w, gn_b = (prm[i:i + 1] for i in range(10))

    r_raw = r_ref[...].astype(F32)
    k_raw = k_ref[...].astype(F32)
    v_raw = v_ref[...].astype(F32)
    l_raw = l_ref[...].astype(F32)
    r = token_shift(r_raw, carry_ref[0:1, 0:aw], mu_r)
    k = token_shift(k_raw, carry_ref[0:1, aw:2 * aw], mu_k)
    v = token_shift(v_raw, carry_ref[0:1, 2 * aw:3 * aw], mu_v)
    xl = token_shift(l_raw, carry_ref[0:1, 3 * aw:3 * aw + lw], mul_ref[0:1])
    carry_ref[0:1, 0:aw] = r_raw[rows - 1:rows]
    carry_ref[0:1, aw:2 * aw] = k_raw[rows - 1:rows]
    carry_ref[0:1, 2 * aw:3 * aw] = v_raw[rows - 1:rows]
    carry_ref[0:1, 3 * aw:3 * aw + lw] = l_raw[rows - 1:rows]

    l0 = xl[:, :V7X_LANES]
    l1 = xl[:, V7X_LANES:]
    dw = _mm(jnp.tanh(l0), du_ref[...])
    da = _mm(l0, iu_ref[...])
    g = _mm(_sigmoid(l1), gu_ref[...])
    w_log = -_softplus(-(w0 + dw)) - 0.5
    logw = -jnp.exp(w_log)
    a = _sigmoid(a0 + da)

    bones = bones_ref[...]
    kk = k * k_k
    ssq = _mm_hilo(kk * kk, bones)
    kk = kk * jnp.minimum(lax.rsqrt(ssq), 1e12)
    k_mod = k * (1.0 + (a - 1.0) * k_a)
    a_vec = -kk
    b_vec = kk * a

    cum_all = _hilo_mm(lt_ref[...], logw)

    row = lax.broadcasted_iota(jnp.int32, (c, c), 0)
    col = lax.broadcasted_iota(jnp.int32, (c, c), 1)
    strict = col < row
    incl = col <= row
    eye = row == col
    same_head = (row // HEAD_DIM) == (col // HEAD_DIM)
    m0 = lax.broadcasted_iota(jnp.int32, (1, V7X_LANES), 1) < HEAD_DIM

    def stack(x):
        return jnp.concatenate([jnp.where(m0, x, 0.0), jnp.where(m0, 0.0, x)], axis=0)

    a_s, a_t, r_s, r_t, b_s, k_s, b_p, k_p, p_end, v_c = ([] for _ in range(10))
    for j in range(n_chunks):
        rs = slice(j * c, (j + 1) * c)
        cum = cum_all[rs]
        cmid = cum[c // 2 - 1:c // 2]
        cend = cum[c - 1:c]
        emid = jnp.exp(cmid)
        a_s.append(a_vec[rs] * jnp.exp(cum - logw[rs] - cmid))
        a_t.append(a_s[j] * emid)
        r_s.append(r[rs] * jnp.exp(cum - cmid))
        r_t.append(r_s[j] * emid)
        e_neg = jnp.exp(cmid - cum)
        b_s.append(b_vec[rs] * e_neg)
        k_s.append(k_mod[rs] * e_neg)
        e_end = jnp.exp(cend - cum)
        b_p.append(b_vec[rs] * e_end)
        k_p.append(k_mod[rs] * e_end)
        p_end.append(jnp.exp(cend))
        v_c.append(v[rs])

    sls = [slice(p * V7X_LANES, (p + 1) * V7X_LANES) for p in range(n_pairs)]
    units = [(j, p) for j in range(n_chunks) for p in range(n_pairs)]
    nu = range(len(units))
    v_p = [v_c[j][:, sls[p]] for j, p in units]
    gm = [_mm_nt(jnp.concatenate([stack(a_s[j][:, sls[p]]), stack(r_s[j][:, sls[p]])], axis=0),
                 jnp.concatenate([b_s[j][:, sls[p]], k_s[j][:, sls[p]]], axis=0)) for j, p in units]
    aab = [jnp.where(strict, gm[u][h * c:(h + 1) * c, 0:c], 0.0) for u in nu for h in range(2)]
    tinv = _tri_inverse_many(aab, row, col)
    aak = [jnp.concatenate([jnp.where(strict, gm[u][h * c:(h + 1) * c, c:2 * c], 0.0) for h in range(2)],
                           axis=1) for u in nu]
    arb = [jnp.concatenate([jnp.where(incl, gm[u][(2 + h) * c:(3 + h) * c, 0:c], 0.0) for h in range(2)],
                           axis=1) for u in nu]
    ark = [jnp.concatenate([jnp.where(incl, gm[u][(2 + h) * c:(3 + h) * c, c:2 * c], 0.0) for h in range(2)],
                           axis=1) for u in nu]
    vs = [stack(x) for x in v_p]
    akv = [_mm(aak[u], vs[u]) for u in nu]
    wu = [_mm(jnp.concatenate([tinv[2 * u], tinv[2 * u + 1]], axis=1),
              jnp.concatenate([stack(a_t[j][:, sls[p]]), stack(akv[u])], axis=1))
          for u, (j, p) in enumerate(units)]
    w_t = [x[:, :V7X_LANES] for x in wu]
    u_v = [x[:, V7X_LANES:] for x in wu]
    q_t = [r_t[j][:, sls[p]] + _mm(arb[u], stack(w_t[u])) for u, (j, p) in enumerate(units)]
    y_v = [_mm(jnp.concatenate([arb[u], ark[u]], axis=1),
               jnp.concatenate([stack(u_v[u]), vs[u]], axis=0)) for u in nu]
    bp_t = [b_p[j][:, sls[p]].T for j, p in units]
    kp_t = [k_p[j][:, sls[p]].T for j, p in units]
    m_mat = [jnp.where(same_head, _mm(bp_t[u], w_t[u]), 0.0) for u in nu]
    n_mat = [jnp.where(same_head, _mm(jnp.concatenate([bp_t[u], kp_t[u]], axis=1),
                                      jnp.concatenate([u_v[u], v_p[u]], axis=0)), 0.0) for u in nu]
    p_col = [jnp.sum(jnp.where(eye, p_end[j][:, sls[p]], 0.0), axis=1, keepdims=True) for j, p in units]
    z = [z_ref[p] for p in range(n_pairs)]
    y_rows = []
    for j in range(n_chunks):
        y_parts = []
        for p in range(n_pairs):
            u = j * n_pairs + p
            y_parts.append(_mm(q_t[u], z[p]) + y_v[u])
            z[p] = p_col[u] * z[p] + _mm(m_mat[u], z[p]) + n_mat[u]
        y_rows.append(jnp.concatenate(y_parts, axis=1))
    for p in range(n_pairs):
        z_ref[p] = z[p]

    y = jnp.concatenate(y_rows, axis=0)
    inv_hd = 1.0 / HEAD_DIM
    mean = _mm_hilo(y, bones) * inv_hd
    d = y - mean
    var = _mm_hilo(d * d, bones) * inv_hd
    yn = d * lax.rsqrt(var + GN_EPS) * gn_w + gn_b
    bonus = _mm_hilo(r * k_mod * r_k, bones) * v
    o_ref[...] = ((yn + bonus) * g).astype(o_ref.dtype)


def _rwkv(r, k, v, l, prm, mul, du, iu, gu, bsz, t_len):
    n, aw = r.shape
    lw = l.shape[1]
    c = RWKV_CHUNK
    n_chunks = math.gcd(RWKV_CHUNKS_PER_STEP, t_len // c)
    rows = n_chunks * c
    nt = t_len // rows
    heads = np.arange(aw) // HEAD_DIM
    bones = jnp.asarray((heads[:, None] == heads[None, :]).astype(np.float32), BF16)
    lt = jnp.asarray(np.kron(np.eye(n_chunks, dtype=np.float32), np.tril(np.ones((c, c), np.float32))), BF16)
    tok = lambda w: pl.BlockSpec((rows, w), lambda b, t: (b * nt + t, 0))
    full = lambda arr: pl.BlockSpec(arr.shape, lambda b, t: (0,) * arr.ndim)
    return pl.pallas_call(
        _rwkv_body,
        grid=(bsz, nt),
        in_specs=[tok(aw), tok(aw), tok(aw), tok(lw), full(prm), full(mul), full(du), full(iu), full(gu),
                  full(bones), full(lt)],
        out_specs=tok(aw),
        out_shape=jax.ShapeDtypeStruct((n, aw), BF16),
        scratch_shapes=[pltpu.VMEM((aw // V7X_LANES, V7X_LANES, V7X_LANES), F32),
                        pltpu.VMEM((V7X_SUBLANES, 3 * aw + lw), F32)],
        compiler_params=_cparams(("parallel", "arbitrary")),
    )(r, k, v, l, prm, mul, du, iu, gu, bones, lt)


def _sb_body(q_ref, k_ref, v_ref, qg_ref, kg_ref, bo_ref, su_ref, o_ref, qm_ref, kn_ref, acc_ref, car_ref):
    t_len = q_ref.shape[0]
    blk = SB_BLOCK
    m0 = lax.broadcasted_iota(jnp.int32, (1, V7X_LANES), 1) < HEAD_DIM
    scale = LOG2_E / math.sqrt(HEAD_DIM)
    inv_hd = 1.0 / HEAD_DIM

    def prep(i, _):
        s0 = pl.multiple_of(i * blk, blk)
        q = q_ref[pl.ds(s0, blk), :].astype(F32)
        qn = q * lax.rsqrt(_mm_hilo(q * q, bo_ref[...]) * inv_hd + RMS_EPS) * (qg_ref[...] * scale)
        qm_ref[0, pl.ds(s0, blk), :] = jnp.where(m0, qn, 0.0).astype(BF16)
        qm_ref[1, pl.ds(s0, blk), :] = jnp.where(m0, 0.0, qn).astype(BF16)
        kx = k_ref[pl.ds(s0, blk), :].astype(F32)
        kn = kx * lax.rsqrt(_mm_hilo(kx * kx, bo_ref[...]) * inv_hd + RMS_EPS) * kg_ref[...]
        kn_ref[pl.ds(s0, blk), :] = kn.astype(BF16)
        return 0

    lax.fori_loop(0, t_len // blk, prep, 0)

    n_qblocks = t_len // blk
    chains = [(qq, h) for qq in range(SB_QUERY_GROUP) for h in range(2)]

    def process(q0, k0, n_kb, diag):
        width = n_kb * blk
        kb = kn_ref[pl.ds(k0, width), :]
        vb = v_ref[pl.ds(k0, width), :]
        z = [_mm_nt(qm_ref[h, pl.ds(q0 + qq * blk, blk), :], kb) for qq, h in chains]
        sp = [jnp.maximum(zc, 0.0) + jnp.log2(1.0 + jnp.exp2(_neg_abs(zc))) for zc in z]
        if diag:
            row = lax.broadcasted_iota(jnp.int32, (blk, width), 0)
            col = lax.broadcasted_iota(jnp.int32, (blk, width), 1)
            causal = [(col + k0) < (row + q0 + qq * blk) for qq in range(SB_QUERY_GROUP)]
            sp = [jnp.where(causal[qq], x, 0.0) for (qq, _), x in zip(chains, sp)]
        sp_lp = [x.astype(BF16) for x in sp]
        su = su_ref[...]
        rs = [[jnp.dot(x[:, b * blk:(b + 1) * blk], su, preferred_element_type=F32) for b in range(n_kb)]
              for x in sp_lp]
        att = []
        for c, (qq, _) in enumerate(chains):
            car = car_ref[c]
            after = [None] * n_kb
            for b in reversed(range(n_kb)):
                after[b] = rs[c][b][:, :blk] + car
                car = car + rs[c][b][:, blk:]
            car_ref[c] = car
            logit = z[c] - sp[c] - jnp.concatenate(after, axis=1)
            if diag:
                logit = jnp.where(causal[qq], logit, NEG_BIG)
            att.append(jnp.exp2(logit))
        for c in range(len(chains)):
            acc_ref[c] += _mm(att[c], vb)

    def sweep_pair(q0, n_wide, half_step):
        acc_ref[...] = jnp.zeros_like(acc_ref)
        car_ref[...] = jnp.zeros_like(car_ref)
        process(q0, q0, SB_QUERY_GROUP, True)
        base = q0
        if half_step:
            base = q0 - SB_QUERY_GROUP * blk
            process(q0, base, SB_QUERY_GROUP, False)
        wide = SB_KEY_GROUP * blk

        def kstep(jj, _):
            process(q0, pl.multiple_of(base - (jj + 1) * wide, wide), SB_KEY_GROUP, False)
            return 0

        lax.fori_loop(0, n_wide, kstep, 0)
        for qq in range(SB_QUERY_GROUP):
            o_ref[pl.ds(q0 + qq * blk, blk), :] = jnp.where(m0, acc_ref[2 * qq], acc_ref[2 * qq + 1]
                                                            ).astype(o_ref.dtype)

    quad = SB_KEY_GROUP * blk
    n_quads = n_qblocks // SB_KEY_GROUP

    def quad_step(m, _):
        q0 = pl.multiple_of(m * quad, quad)
        sweep_pair(q0, m, False)
        sweep_pair(q0 + SB_QUERY_GROUP * blk, m, True)
        return 0

    lax.fori_loop(0, n_quads, quad_step, 0)
    if n_qblocks % SB_KEY_GROUP:
        sweep_pair(n_quads * quad, n_quads, False)


def _stick_breaking(q, k, v, qg, kg, bsz, t_len):
    n, bw = q.shape
    n_pairs = bw // V7X_LANES
    blk = SB_BLOCK
    heads = np.arange(V7X_LANES) // HEAD_DIM
    bo = jnp.asarray((heads[:, None] == heads[None, :]).astype(np.float32), BF16)
    idx = np.arange(blk)
    su = np.concatenate([(idx[:, None] > idx[None, :]).astype(np.float32), np.ones((blk, blk), np.float32)], axis=1)
    su = jnp.asarray(su, BF16)
    tok = pl.BlockSpec((t_len, V7X_LANES), lambda b, p: (b, p))
    full = lambda arr: pl.BlockSpec(arr.shape, lambda b, p: (0,) * arr.ndim)
    assert SB_KEY_GROUP == 2 * SB_QUERY_GROUP and t_len % (SB_QUERY_GROUP * blk) == 0
    qgroup = SB_QUERY_GROUP
    return pl.pallas_call(
        _sb_body,
        grid=(bsz, n_pairs),
        in_specs=[tok, tok, tok, full(qg), full(kg), full(bo), full(su)],
        out_specs=tok,
        out_shape=jax.ShapeDtypeStruct((n, bw), BF16),
        scratch_shapes=[pltpu.VMEM((2, t_len, V7X_LANES), BF16),
                        pltpu.VMEM((t_len, V7X_LANES), BF16),
                        pltpu.VMEM((2 * qgroup, blk, V7X_LANES), F32),
                        pltpu.VMEM((2 * qgroup, blk, blk), F32)],
        compiler_params=_cparams(("parallel", "parallel")),
    )(q, k, v, qg, kg, bo, su)


def _lru_body(gate_ref, rec_ref, cw_ref, pv_ref, wr_ref, wi_ref, o_ref, prev_ref, h_ref):
    tt = rec_ref.shape[0]
    width = rec_ref.shape[1]
    n_blocks = width // V7X_LANES

    @pl.when(pl.program_id(1) == 0)
    def _():
        prev_ref[...] = jnp.zeros_like(prev_ref)
        h_ref[...] = jnp.zeros_like(h_ref)

    row8 = lax.broadcasted_iota(jnp.int32, (V7X_SUBLANES, 1), 0)

    for gi in range(n_blocks):
        sl = slice(gi * V7X_LANES, (gi + 1) * V7X_LANES)
        u = rec_ref[:, sl].astype(F32)
        prev = prev_ref[:, sl]

        def shifted(s):
            ru = pltpu.roll(u, s, 0)
            first = jnp.where(row8 < s, pltpu.roll(prev, s, 0), ru[0:V7X_SUBLANES])
            return jnp.concatenate([first, ru[V7X_SUBLANES:]], axis=0)

        cw = cw_ref[:, sl]
        rc = (cw[3:4] * u + cw[2:3] * shifted(1) + cw[1:2] * shifted(2) + cw[0:1] * shifted(3)
              + pv_ref[0:1, sl])
        prev_ref[:, sl] = u[tt - V7X_SUBLANES:tt]

        rgate = _sigmoid(_mm(rc, wr_ref[gi]) + pv_ref[1:2, sl])
        igate = _sigmoid(_mm(rc, wi_ref[gi]) + pv_ref[2:3, sl])
        log_a = -LRU_C * rgate * _softplus(-pv_ref[3:4, sl])
        a = jnp.exp(log_a)
        om = 1.0 - a * a
        b = jnp.where(om > 0.0, om * lax.rsqrt(om), 0.0) * (igate * rc)

        s = 1
        while s < V7X_SUBLANES:
            ra = pltpu.roll(a, s, 0)
            rb = pltpu.roll(b, s, 0)
            a_sh = jnp.concatenate([jnp.where(row8 >= s, ra[0:V7X_SUBLANES], 1.0), ra[V7X_SUBLANES:]], axis=0)
            b_sh = jnp.concatenate([jnp.where(row8 >= s, rb[0:V7X_SUBLANES], 0.0), rb[V7X_SUBLANES:]], axis=0)
            b = a * b_sh + b
            a = a * a_sh
            s *= 2
        n_groups = tt // V7X_SUBLANES
        ag = [a[j * V7X_SUBLANES:(j + 1) * V7X_SUBLANES] for j in range(n_groups)]
        bg = [b[j * V7X_SUBLANES:(j + 1) * V7X_SUBLANES] for j in range(n_groups)]
        k = 1
        while k < n_groups:
            bg = [bg[j] if j < k else ag[j] * bg[j - k] + bg[j] for j in range(n_groups)]
            ag = [ag[j] if j < k else ag[j] * ag[j - k] for j in range(n_groups)]
            k *= 2
        h_in = h_ref[0:1, sl]
        h = jnp.concatenate([ag[j] * h_in + bg[j] for j in range(n_groups)], axis=0)
        h_ref[0:1, sl] = h[tt - 1:tt]
        o_ref[:, sl] = (jax.nn.gelu(gate_ref[:, sl].astype(F32)) * h).astype(o_ref.dtype)


def _rg_lru(gate, rec, cw, pv, wr, wi, bsz, t_len, tt):
    n, width = rec.shape
    nt = t_len // tt
    tok = pl.BlockSpec((tt, width), lambda b, t: (b * nt + t, 0))
    full = lambda arr: pl.BlockSpec(arr.shape, lambda b, t: (0,) * arr.ndim)
    return pl.pallas_call(
        _lru_body,
        grid=(bsz, nt),
        in_specs=[tok, tok, full(cw), full(pv), full(wr), full(wi)],
        out_specs=tok,
        out_shape=jax.ShapeDtypeStruct((n, width), BF16),
        scratch_shapes=[pltpu.VMEM((V7X_SUBLANES, width), F32), pltpu.VMEM((V7X_SUBLANES, width), F32)],
        compiler_params=_cparams(("parallel", "arbitrary")),
    )(gate, rec, cw, pv, wr, wi)


def _proj_norm_route_body(*refs, n_in):
    x_ref = refs[0]
    a_refs = refs[1:1 + n_in]
    w_refs = refs[1 + n_in:1 + 2 * n_in]
    g_ref, whl_ref, b_ref, slt_ref, xo_ref, h_ref, rt_ref, cnt_ref, run_ref = refs[1 + 2 * n_in:]

    @pl.when(pl.program_id(0) == 0)
    def _():
        run_ref[...] = jnp.zeros_like(run_ref)

    x = x_ref[...]
    for a_ref, w_ref in zip(a_refs, w_refs):
        x = x + jnp.dot(a_ref[...], w_ref[...], preferred_element_type=F32)
    xo_ref[...] = x
    ms = jnp.mean(x * x, axis=-1, keepdims=True)
    h = x * lax.rsqrt(ms + RMS_EPS) * g_ref[...]
    h_hi, h_lo = _split_hilo(h)
    h_ref[...] = h_hi
    both = jnp.dot(h_hi, whl_ref[...], preferred_element_type=F32)
    lg = (both[:, :V7X_LANES] + both[:, V7X_LANES:]
          + jnp.dot(h_lo, whl_ref[:, :V7X_LANES], preferred_element_type=F32)) + b_ref[0:1]
    lane = lax.broadcasted_iota(jnp.int32, lg.shape, 1)
    neg_inf = -jnp.inf
    is_g = lane < N_GROUPS
    gl = jnp.where(is_g, lg, neg_inf)
    gmax = jnp.max(gl, axis=-1, keepdims=True)
    gidx = jnp.min(jnp.where(gl == gmax, lane, V7X_LANES), axis=-1, keepdims=True)
    g_top = 1.0 / jnp.sum(jnp.where(is_g, jnp.exp(gl - gmax), 0.0), axis=-1, keepdims=True)
    lo_e = N_GROUPS + EXPERTS_PER_GROUP * gidx
    el = jnp.where((lane >= lo_e) & (lane < lo_e + EXPERTS_PER_GROUP), lg, neg_inf)
    m1 = jnp.max(el, axis=-1, keepdims=True)
    i1 = jnp.min(jnp.where(el == m1, lane, V7X_LANES), axis=-1, keepdims=True)
    el2 = jnp.where(lane == i1, neg_inf, el)
    m2 = jnp.max(el2, axis=-1, keepdims=True)
    i2 = jnp.min(jnp.where(el2 == m2, lane, V7X_LANES), axis=-1, keepdims=True)
    t = jnp.exp(m2 - m1)
    gate1 = g_top / (1.0 + t)
    gate2 = g_top * t / (1.0 + t)
    oh1 = lane == i1
    oh2 = lane == i2
    oh = jnp.where(oh1 | oh2, 1.0, 0.0)
    before = jnp.dot(slt_ref[...], oh.astype(BF16), preferred_element_type=F32) + run_ref[0:1]
    rank1 = jnp.sum(jnp.where(oh1, before, 0.0), axis=-1, keepdims=True)
    rank2 = jnp.sum(jnp.where(oh2, before, 0.0), axis=-1, keepdims=True)
    run_ref[0:1] = run_ref[0:1] + jnp.sum(oh, axis=0, keepdims=True)
    cnt_ref[...] = jnp.broadcast_to(run_ref[0:1], cnt_ref.shape)
    vals = (i1 - N_GROUPS, i2 - N_GROUPS, rank1, rank2, gate1, gate2)
    out = jnp.zeros(lg.shape, F32)
    for li, val in enumerate(vals):
        out = jnp.where(lane == li, val.astype(F32), out)
    rt_ref[...] = out.T[0:V7X_SUBLANES]


def _proj_norm_route(x, acts, ws, g, w_router, b_router, tm):
    n, d = x.shape
    n_in = len(acts)
    nr = w_router.shape[1]
    wpad = jnp.zeros((d, V7X_LANES), F32).at[:, :nr].set(w_router)
    wh = wpad.astype(BF16)
    whl = jnp.concatenate([wh, (wpad - wh.astype(F32)).astype(BF16)], axis=1)
    bias = jnp.zeros((V7X_SUBLANES, V7X_LANES), F32).at[0, :nr].set(b_router)
    slt = jnp.asarray(np.tril(np.ones((tm, tm), np.float32), -1), BF16)
    const = lambda shape: pl.BlockSpec(shape, lambda i: (0, 0))
    rows = lambda width: pl.BlockSpec((tm, width), lambda i: (i, 0))
    return pl.pallas_call(
        functools.partial(_proj_norm_route_body, n_in=n_in),
        grid=(n // tm,),
        in_specs=([rows(d)] + [rows(a.shape[1]) for a in acts] + [const(w.shape) for w in ws]
                  + [const((1, d)), const((d, 2 * V7X_LANES)),
                     const((V7X_SUBLANES, V7X_LANES)), const((tm, tm))]),
        out_specs=[rows(d), rows(d), pl.BlockSpec((V7X_SUBLANES, tm), lambda i: (0, i)),
                   const((V7X_SUBLANES, V7X_LANES))],
        out_shape=[jax.ShapeDtypeStruct((n, d), F32), jax.ShapeDtypeStruct((n, d), BF16),
                   jax.ShapeDtypeStruct((V7X_SUBLANES, n), F32),
                   jax.ShapeDtypeStruct((V7X_SUBLANES, V7X_LANES), F32)],
        scratch_shapes=[pltpu.VMEM((V7X_SUBLANES, V7X_LANES), F32)],
        compiler_params=_cparams(("arbitrary",)),
    )(x, *acts, *ws, g, whl, bias, slt)


def _moe_body(te_ref, nv_ref, x_ref, wg_ref, wu_ref, wd_ref, o_ref, wgb_ref, wub_ref, wdb_ref):
    i = pl.program_id(0)
    valid = i < nv_ref[0]

    @pl.when(valid & ((i == 0) | (te_ref[i] != te_ref[jnp.maximum(i - 1, 0)])))
    def _():
        wgb_ref[...] = wg_ref[0].astype(BF16)
        wub_ref[...] = wu_ref[0].astype(BF16)
        wdb_ref[...] = wd_ref[0].astype(BF16)

    @pl.when(valid)
    def _():
        x = x_ref[...]
        gt = jnp.dot(x, wgb_ref[...], preferred_element_type=F32)
        up = jnp.dot(x, wub_ref[...], preferred_element_type=F32)
        hid = (gt * _sigmoid(gt)) * up
        o_ref[...] = jnp.dot(hid.astype(BF16), wdb_ref[...], preferred_element_type=F32).astype(o_ref.dtype)


def _moe_grouped(xs, tile_expert, n_valid, wg, wu, wd, layer):
    p_rows, d = xs.shape
    de = wg.shape[3]
    tm = MOE_TILE
    grid_spec = pltpu.PrefetchScalarGridSpec(
        num_scalar_prefetch=2,
        grid=(p_rows // tm,),
        in_specs=[pl.BlockSpec((tm, d), lambda i, te, nv: (i, 0)),
                  pl.BlockSpec((None, 1, d, de), lambda i, te, nv: (layer, te[i], 0, 0)),
                  pl.BlockSpec((None, 1, d, de), lambda i, te, nv: (layer, te[i], 0, 0)),
                  pl.BlockSpec((None, 1, de, d), lambda i, te, nv: (layer, te[i], 0, 0))],
        out_specs=pl.BlockSpec((tm, d), lambda i, te, nv: (i, 0)),
        scratch_shapes=[pltpu.VMEM((d, de), BF16), pltpu.VMEM((d, de), BF16), pltpu.VMEM((de, d), BF16)],
    )
    return pl.pallas_call(
        _moe_body,
        grid_spec=grid_spec,
        out_shape=jax.ShapeDtypeStruct((p_rows, d), BF16),
        compiler_params=_cparams(("arbitrary",)),
    )(tile_expert, n_valid, xs, wg, wu, wd)


def _mix_out_moe(x, acts, ws, g_norm, w_group, b_group, w_erouter, b_erouter, wg, wu, wd, layer, tm):
    n, d = x.shape
    x, hn, route, cnt = _proj_norm_route(x, acts, ws, g_norm, jnp.concatenate([w_group, w_erouter], axis=1),
                                         jnp.concatenate([b_group, b_erouter]), tm)
    expert_id = route[0:2].astype(jnp.int32)
    rank = route[2:4].astype(jnp.int32)
    gates = route[4:6]
    counts = cnt[0, N_GROUPS:N_GROUPS + N_EXPERTS].astype(jnp.int32)

    tmm = MOE_TILE
    padded = ((counts + tmm - 1) // tmm) * tmm
    ends = jnp.cumsum(padded)
    starts = ends - padded
    start_of = jnp.zeros_like(expert_id)
    for e in range(N_EXPERTS):
        start_of = jnp.where(expert_id == e, starts[e], start_of)
    pos = start_of + rank
    p_rows = 2 * n + N_EXPERTS * tmm
    n_tiles = p_rows // tmm
    tok = jnp.broadcast_to(jnp.arange(n, dtype=jnp.int32)[None, :], (2, n))
    row_token = (jnp.arange(p_rows, dtype=jnp.int32) % n).at[pos.reshape(-1)].set(
        tok.reshape(-1), mode="promise_in_bounds", unique_indices=True)
    tile_expert = jnp.minimum(
        jnp.searchsorted(ends, jnp.arange(n_tiles, dtype=jnp.int32) * tmm, side="right"), N_EXPERTS - 1
    ).astype(jnp.int32)
    n_valid = (ends[-1] // tmm).astype(jnp.int32).reshape(1)

    take_rows = lambda a, idx: jnp.take(a, idx, axis=0, mode="clip")
    xs = take_rows(hn, row_token)
    ys = _moe_grouped(xs, tile_expert, n_valid, wg, wu, wd, layer)
    y = (gates[0][:, None] * take_rows(ys, pos[0]).astype(F32)
         + gates[1][:, None] * take_rows(ys, pos[1]).astype(F32))
    return x + y


def _pad_rows(w, rows, offset=0):
    out = jnp.zeros((rows, w.shape[1]), w.dtype)
    return out.at[offset:offset + w.shape[0]].set(w)


def kernel(x, norm_mix, norm_ffn, w_in_even, mu_a, w0, decay_up, a0, iclr_up, gate_up, k_k, k_a, r_k, gn_w,
           gn_b, q_norm_g, k_norm_g, w_out_even, w_in_odd, conv_w, conv_b, w_rgate, b_rgate, w_igate,
           b_igate, lru_lambda, w_out_odd, w_group, b_group, w_erouter, b_erouter, exp_w_gate, exp_w_up,
           exp_w_down):
    bsz, t_len, d = x.shape
    n = bsz * t_len
    depth = norm_mix.shape[0]
    aw = decay_up.shape[2]
    d_lora, i_lora, g_lora = decay_up.shape[1], iclr_up.shape[1], gate_up.shape[1]
    bw = (w_in_even.shape[2] - (3 * aw + d_lora + i_lora + g_lora)) // 3
    lw = w_rgate.shape[1] * w_rgate.shape[2]
    assert d_lora + i_lora == V7X_LANES and g_lora <= 2 * V7X_LANES
    lora_w = 3 * V7X_LANES
    tm = 512
    xf = x.reshape(n, d)

    for layer in range(depth):
        gm = norm_mix[layer].reshape(1, d)
        if layer % 2 == 0:
            i = layer // 2
            w_in = w_in_even[i]
            o3 = 3 * aw
            a_cols = o3 + d_lora + i_lora + g_lora
            pad = jnp.zeros((d, lora_w - (d_lora + i_lora + g_lora)), F32)
            w_cat = jnp.concatenate([w_in[:, :a_cols], pad, w_in[:, a_cols:]], axis=1).astype(BF16)
            r, k, v, lo, qb, kb, vb = _norm_matmul(xf, gm, w_cat, (aw, aw, aw, lora_w, bw, bw, bw), tm)
            mu = mu_a[i]
            rows = [mu[:aw], mu[aw:2 * aw], mu[2 * aw:o3], w0[i], a0[i], k_k[i], k_a[i], r_k[i].reshape(-1),
                    gn_w[i], gn_b[i]]
            prm = jnp.zeros((16, aw), F32).at[:len(rows)].set(jnp.stack(rows))
            mul = jnp.zeros((V7X_SUBLANES, lora_w), F32).at[0, :a_cols - o3].set(mu[o3:])
            du = _pad_rows(decay_up[i], V7X_LANES, 0).astype(BF16)
            iu = _pad_rows(iclr_up[i], V7X_LANES, d_lora).astype(BF16)
            gu = _pad_rows(gate_up[i], 2 * V7X_LANES, 0).astype(BF16)
            ya = _rwkv(r, k, v, lo, prm, mul, du, iu, gu, bsz, t_len)
            qg = jnp.tile(q_norm_g[i], V7X_LANES // HEAD_DIM).reshape(1, V7X_LANES)
            kg = jnp.tile(k_norm_g[i], V7X_LANES // HEAD_DIM).reshape(1, V7X_LANES)
            yb = _stick_breaking(qb, kb, vb, qg, kg, bsz, t_len)
            w_out = w_out_even[i].astype(BF16)
            acts, ws = [ya, yb], [w_out[:aw], w_out[aw:]]
        else:
            j = layer // 2
            gate, rec = _norm_matmul(xf, gm, w_in_odd[j].astype(BF16), (lw, lw), tm)
            cw = jnp.zeros((V7X_SUBLANES, lw), F32).at[:conv_w.shape[1]].set(conv_w[j])
            pv = jnp.zeros((V7X_SUBLANES, lw), F32).at[:4].set(
                jnp.stack([conv_b[j], b_rgate[j], b_igate[j], lru_lambda[j]]))
            yl = _rg_lru(gate, rec, cw, pv, w_rgate[j].astype(BF16), w_igate[j].astype(BF16), bsz, t_len,
                         min(256, t_len))
            acts, ws = [yl], [w_out_odd[j].astype(BF16)]
        xf = _mix_out_moe(xf, acts, ws, norm_ffn[layer].reshape(1, d), w_group[layer], b_group[layer],
                          w_erouter[layer], b_erouter[layer], exp_w_gate, exp_w_up, exp_w_down, layer, tm)
    return xf.reshape(bsz, t_len, d)
```

```python
import functools
import math

import numpy as np
import jax
import jax.numpy as jnp
from jax import lax
from jax.experimental import pallas as pl
from jax.experimental.pallas import tpu as pltpu

F32 = jnp.float32
BF16 = jnp.bfloat16

V7X_LANES = 128
V7X_SUBLANES = 8
V7X_VMEM_LIMIT_BYTES = 48 * 1024 * 1024

HEAD_DIM = 64
RMS_EPS = 1e-6
GN_EPS = 64e-5
LRU_C = 8.0
N_GROUPS = 4
EXPERTS_PER_GROUP = 4
N_EXPERTS = 16
RWKV_CHUNK = 128
RWKV_CHUNKS_PER_STEP = 2
SB_BLOCK = 128
SB_KEY_GROUP = 4
SB_QUERY_GROUP = 2
MOE_TILE = 512
NEG_BIG = -1e30
LOG2_E = 1.4426950408889634


def _cparams(sem):
    return pltpu.CompilerParams(dimension_semantics=sem, vmem_limit_bytes=V7X_VMEM_LIMIT_BYTES)


def _mm(a, b):
    return jnp.dot(a.astype(BF16), b.astype(BF16), preferred_element_type=F32)


def _mm_nt(a, b):
    return lax.dot_general(a.astype(BF16), b.astype(BF16), (((1,), (1,)), ((), ())),
                           preferred_element_type=F32)


def _split_hilo(x):
    hi = x.astype(BF16)
    lo = (x - hi.astype(F32)).astype(BF16)
    return hi, lo


def _mm_hilo(x, m):
    hi, lo = _split_hilo(x)
    return (jnp.dot(hi, m, preferred_element_type=F32) + jnp.dot(lo, m, preferred_element_type=F32))


def _hilo_mm(m, x):
    hi, lo = _split_hilo(x)
    return (jnp.dot(m, hi, preferred_element_type=F32) + jnp.dot(m, lo, preferred_element_type=F32))


def _sigmoid(x):
    return 0.5 * jnp.tanh(0.5 * x) + 0.5


def _neg_abs(x):
    bits = pltpu.bitcast(x, jnp.uint32) | jnp.uint32(0x80000000)
    return pltpu.bitcast(bits, F32)


def _softplus(x):
    return jnp.maximum(x, 0.0) + jnp.log(1.0 + jnp.exp(-jnp.abs(x)))


def _norm_matmul_body(x_ref, g_ref, w_ref, *rest, col_splits, head_norm):
    if head_norm:
        bones_ref, hg_ref = rest[:2]
        out_refs = rest[2:]
    else:
        out_refs = rest
    x = x_ref[...]
    ms = jnp.mean(x * x, axis=-1, keepdims=True)
    h = (x * lax.rsqrt(ms + RMS_EPS) * g_ref[...]).astype(BF16)
    off = 0
    for gi, (o_ref, c) in enumerate(zip(out_refs, col_splits)):
        y = jnp.dot(h, w_ref[:, off:off + c], preferred_element_type=F32)
        if gi in head_norm:
            row = head_norm[gi]
            msq = jnp.dot((y * y).astype(BF16), bones_ref[...], preferred_element_type=F32) * (1.0 / HEAD_DIM)
            y = y * lax.rsqrt(msq + RMS_EPS) * hg_ref[row:row + 1]
        o_ref[...] = y.astype(o_ref.dtype)
        off += c


def _norm_matmul(x, g, w, col_splits, tm, head_norm=None, head_gains=None):
    n, d = x.shape
    ctot = w.shape[1]
    assert sum(col_splits) == ctot and n % tm == 0
    head_norm = dict(head_norm or {})
    extra, extra_specs = [], []
    if head_norm:
        width = col_splits[next(iter(head_norm))]
        assert all(col_splits[gi] == width for gi in head_norm)
        heads = np.arange(width) // HEAD_DIM
        extra = [jnp.asarray((heads[:, None] == heads[None, :]).astype(np.float32), BF16), head_gains]
        extra_specs = [pl.BlockSpec((width, width), lambda i: (0, 0)),
                       pl.BlockSpec(head_gains.shape, lambda i: (0, 0))]
    return pl.pallas_call(
        functools.partial(_norm_matmul_body, col_splits=tuple(col_splits), head_norm=head_norm),
        grid=(n // tm,),
        in_specs=[pl.BlockSpec((tm, d), lambda i: (i, 0)),
                  pl.BlockSpec((1, d), lambda i: (0, 0)),
                  pl.BlockSpec((d, ctot), lambda i: (0, 0))] + extra_specs,
        out_specs=[pl.BlockSpec((tm, c), lambda i: (i, 0)) for c in col_splits],
        out_shape=[jax.ShapeDtypeStruct((n, c), BF16) for c in col_splits],
        compiler_params=_cparams(("parallel",)),
    )(x, g, w, *extra)


def _tri_inverse_many(mats, row, col):
    eye = (row == col).astype(F32)
    blk8 = (row // 8) == (col // 8)
    n1 = [jnp.where(blk8, a, 0.0) for a in mats]
    n2 = [_mm(n, n) for n in n1]
    ipn = [eye + n for n in n1]
    p = [i + _mm(i, m) for i, m in zip(ipn, n2)]
    n4 = [_mm(m, m) for m in n2]
    x = [q + _mm(q, m) for q, m in zip(p, n4)]
    k = 8
    c = mats[0].shape[0]
    while k < c:
        sel = ((row // (2 * k)) == (col // (2 * k))) & ((row // k) > (col // k))
        e = [jnp.where(sel, a, 0.0) for a in mats]
        xe = [_mm(xi, ei) for xi, ei in zip(x, e)]
        x = [xi + _mm(xei, xi) for xi, xei in zip(x, xe)]
        k *= 2
    return x


def _rwkv_body(r_ref, k_ref, v_ref, l_ref, p_ref, mul_ref, du_ref, iu_ref, gu_ref, bones_ref, lt_ref,
               o_ref, z_ref, carry_ref):
    c = RWKV_CHUNK
    rows = r_ref.shape[0]
    n_chunks = rows // c
    aw = r_ref.shape[1]
    lw = l_ref.shape[1]
    n_pairs = aw // V7X_LANES

    @pl.when(pl.program_id(1) == 0)
    def _():
        z_ref[...] = jnp.zeros_like(z_ref)
        carry_ref[...] = jnp.zeros_like(carry_ref)

    row1 = lax.broadcasted_iota(jnp.int32, (rows, 1), 0)

    def token_shift(raw, prev_last, mu):
        prev = jnp.where(row1 == 0, prev_last, pltpu.roll(raw, 1, 0))
        return raw + mu * (prev - raw)

    prm = p_ref[...]
    mu_r, mu_k, mu_v, w0, a0, k_k, k_a, r_k, gn_w, gn_b = (prm[i:i + 1] for i in range(10))

    r_raw = r_ref[...].astype(F32)
    k_raw = k_ref[...].astype(F32)
    v_raw = v_ref[...].astype(F32)
    l_raw = l_ref[...].astype(F32)
    r = token_shift(r_raw, carry_ref[0:1, 0:aw], mu_r)
    k = token_shift(k_raw, carry_ref[0:1, aw:2 * aw], mu_k)
    v = token_shift(v_raw, carry_ref[0:1, 2 * aw:3 * aw], mu_v)
    xl = token_shift(l_raw, carry_ref[0:1, 3 * aw:3 * aw + lw], mul_ref[0:1])
    carry_ref[0:1, 0:aw] = r_raw[rows - 1:rows]
    carry_ref[0:1, aw:2 * aw] = k_raw[rows - 1:rows]
    carry_ref[0:1, 2 * aw:3 * aw] = v_raw[rows - 1:rows]
    carry_ref[0:1, 3 * aw:3 * aw + lw] = l_raw[rows - 1:rows]

    l0 = xl[:, :V7X_LANES]
    l1 = xl[:, V7X_LANES:]
    dw = _mm(jnp.tanh(l0), du_ref[...])
    da = _mm(l0, iu_ref[...])
    g = _mm(_sigmoid(l1), gu_ref[...])
    w_log = -_softplus(-(w0 + dw)) - 0.5
    logw = -jnp.exp(w_log)
    a = _sigmoid(a0 + da)

    bones = bones_ref[...]
    kk = k * k_k
    ssq = _mm_hilo(kk * kk, bones)
    kk = kk * jnp.minimum(lax.rsqrt(ssq), 1e12)
    k_mod = k * (1.0 + (a - 1.0) * k_a)
    a_vec = -kk
    b_vec = kk * a

    cum_all = _hilo_mm(lt_ref[...], logw)

    row = lax.broadcasted_iota(jnp.int32, (c, c), 0)
    col = lax.broadcasted_iota(jnp.int32, (c, c), 1)
    strict = col < row
    incl = col <= row
    eye = row == col
    same_head = (row // HEAD_DIM) == (col // HEAD_DIM)
    m0 = lax.broadcasted_iota(jnp.int32, (1, V7X_LANES), 1) < HEAD_DIM

    def stack(x):
        return jnp.concatenate([jnp.where(m0, x, 0.0), jnp.where(m0, 0.0, x)], axis=0)

    a_s, a_t, r_s, r_t, b_s, k_s, b_p, k_p, p_end, v_c = ([] for _ in range(10))
    for j in range(n_chunks):
        rs = slice(j * c, (j + 1) * c)
        cum = cum_all[rs]
        cmid = cum[c // 2 - 1:c // 2]
        cend = cum[c - 1:c]
        emid = jnp.exp(cmid)
        a_s.append(a_vec[rs] * jnp.exp(cum - logw[rs] - cmid))
        a_t.append(a_s[j] * emid)
        r_s.append(r[rs] * jnp.exp(cum - cmid))
        r_t.append(r_s[j] * emid)
        e_neg = jnp.exp(cmid - cum)
        b_s.append(b_vec[rs] * e_neg)
        k_s.append(k_mod[rs] * e_neg)
        e_end = jnp.exp(cend - cum)
        b_p.append(b_vec[rs] * e_end)
        k_p.append(k_mod[rs] * e_end)
        p_end.append(jnp.exp(cend))
        v_c.append(v[rs])

    sls = [slice(p * V7X_LANES, (p + 1) * V7X_LANES) for p in range(n_pairs)]
    units = [(j, p) for j in range(n_chunks) for p in range(n_pairs)]
    nu = range(len(units))
    v_p = [v_c[j][:, sls[p]] for j, p in units]
    gm = [_mm_nt(jnp.concatenate([stack(a_s[j][:, sls[p]]), stack(r_s[j][:, sls[p]])], axis=0),
                 jnp.concatenate([b_s[j][:, sls[p]], k_s[j][:, sls[p]]], axis=0)) for j, p in units]
    aab = [jnp.where(strict, gm[u][h * c:(h + 1) * c, 0:c], 0.0) for u in nu for h in range(2)]
    tinv = _tri_inverse_many(aab, row, col)
    aak = [jnp.concatenate([jnp.where(strict, gm[u][h * c:(h + 1) * c, c:2 * c], 0.0) for h in range(2)],
                           axis=1) for u in nu]
    arb = [jnp.concatenate([jnp.where(incl, gm[u][(2 + h) * c:(3 + h) * c, 0:c], 0.0) for h in range(2)],
                           axis=1) for u in nu]
    ark = [jnp.concatenate([jnp.where(incl, gm[u][(2 + h) * c:(3 + h) * c, c:2 * c], 0.0) for h in range(2)],
                           axis=1) for u in nu]
    vs = [stack(x) for x in v_p]
    akv = [_mm(aak[u], vs[u]) for u in nu]
    wu = [_mm(jnp.concatenate([tinv[2 * u], tinv[2 * u + 1]], axis=1),
              jnp.concatenate([stack(a_t[j][:, sls[p]]), stack(akv[u])], axis=1))
          for u, (j, p) in enumerate(units)]
    w_t = [x[:, :V7X_LANES] for x in wu]
    u_v = [x[:, V7X_LANES:] for x in wu]
    q_t = [r_t[j][:, sls[p]] + _mm(arb[u], stack(w_t[u])) for u, (j, p) in enumerate(units)]
    y_v = [_mm(jnp.concatenate([arb[u], ark[u]], axis=1),
               jnp.concatenate([stack(u_v[u]), vs[u]], axis=0)) for u in nu]
    bp_t = [b_p[j][:, sls[p]].T for j, p in units]
    kp_t = [k_p[j][:, sls[p]].T for j, p in units]
    m_mat = [jnp.where(same_head, _mm(bp_t[u], w_t[u]), 0.0) for u in nu]
    n_mat = [jnp.where(same_head, _mm(jnp.concatenate([bp_t[u], kp_t[u]], axis=1),
                                      jnp.concatenate([u_v[u], v_p[u]], axis=0)), 0.0) for u in nu]
    p_col = [jnp.sum(jnp.where(eye, p_end[j][:, sls[p]], 0.0), axis=1, keepdims=True) for j, p in units]
    z = [z_ref[p] for p in range(n_pairs)]
    y_rows = []
    for j in range(n_chunks):
        y_parts = []
        for p in range(n_pairs):
            u = j * n_pairs + p
            y_parts.append(_mm(q_t[u], z[p]) + y_v[u])
            z[p] = p_col[u] * z[p] + _mm(m_mat[u], z[p]) + n_mat[u]
        y_rows.append(jnp.concatenate(y_parts, axis=1))
    for p in range(n_pairs):
        z_ref[p] = z[p]

    y = jnp.concatenate(y_rows, axis=0)
    inv_hd = 1.0 / HEAD_DIM
    mean = _mm_hilo(y, bones) * inv_hd
    d = y - mean
    var = _mm_hilo(d * d, bones) * inv_hd
    yn = d * lax.rsqrt(var + GN_EPS) * gn_w + gn_b
    bonus = _mm_hilo(r * k_mod * r_k, bones) * v
    o_ref[...] = ((yn + bonus) * g).astype(o_ref.dtype)


def _rwkv(r, k, v, l, prm, mul, du, iu, gu, bsz, t_len):
    n, aw = r.shape
    lw = l.shape[1]
    c = RWKV_CHUNK
    n_chunks = math.gcd(RWKV_CHUNKS_PER_STEP, t_len // c)
    rows = n_chunks * c
    nt = t_len // rows
    heads = np.arange(aw) // HEAD_DIM
    bones = jnp.asarray((heads[:, None] == heads[None, :]).astype(np.float32), BF16)
    lt = jnp.asarray(np.kron(np.eye(n_chunks, dtype=np.float32), np.tril(np.ones((c, c), np.float32))), BF16)
    tok = lambda w: pl.BlockSpec((rows, w), lambda b, t: (b * nt + t, 0))
    full = lambda arr: pl.BlockSpec(arr.shape, lambda b, t: (0,) * arr.ndim)
    return pl.pallas_call(
        _rwkv_body,
        grid=(bsz, nt),
        in_specs=[tok(aw), tok(aw), tok(aw), tok(lw), full(prm), full(mul), full(du), full(iu), full(gu),
                  full(bones), full(lt)],
        out_specs=tok(aw),
        out_shape=jax.ShapeDtypeStruct((n, aw), BF16),
        scratch_shapes=[pltpu.VMEM((aw // V7X_LANES, V7X_LANES, V7X_LANES), F32),
                        pltpu.VMEM((V7X_SUBLANES, 3 * aw + lw), F32)],
        compiler_params=_cparams(("parallel", "arbitrary")),
    )(r, k, v, l, prm, mul, du, iu, gu, bones, lt)


def _sb_body(q_ref, k_ref, v_ref, su_ref, o_ref, acc_ref, car_ref):
    t_len = q_ref.shape[0]
    blk = SB_BLOCK
    m0 = lax.broadcasted_iota(jnp.int32, (1, V7X_LANES), 1) < HEAD_DIM
    head_mask = (m0, jnp.logical_not(m0))

    n_qblocks = t_len // blk
    chains = [(qq, h) for qq in range(SB_QUERY_GROUP) for h in range(2)]

    def process(q0, k0, n_kb, diag):
        width = n_kb * blk
        kb = k_ref[pl.ds(k0, width), :]
        vb = v_ref[pl.ds(k0, width), :]
        qb = [q_ref[pl.ds(q0 + qq * blk, blk), :] for qq in range(SB_QUERY_GROUP)]
        z = [_mm_nt(jnp.where(head_mask[h], qb[qq], jnp.zeros_like(qb[qq])), kb) for qq, h in chains]
        sp = [jnp.maximum(zc, 0.0) + jnp.log2(1.0 + jnp.exp2(_neg_abs(zc))) for zc in z]
        if diag:
            row = lax.broadcasted_iota(jnp.int32, (blk, width), 0)
            col = lax.broadcasted_iota(jnp.int32, (blk, width), 1)
            causal = [(col + k0) < (row + q0 + qq * blk) for qq in range(SB_QUERY_GROUP)]
            sp = [jnp.where(causal[qq], x, 0.0) for (qq, _), x in zip(chains, sp)]
        sp_lp = [x.astype(BF16) for x in sp]
        su = su_ref[...]
        rs = [[jnp.dot(x[:, b * blk:(b + 1) * blk], su, preferred_element_type=F32) for b in range(n_kb)]
              for x in sp_lp]
        att = []
        for c, (qq, _) in enumerate(chains):
            car = car_ref[c]
            after = [None] * n_kb
            for b in reversed(range(n_kb)):
                after[b] = rs[c][b][:, :blk] + car
                car = car + rs[c][b][:, blk:]
            car_ref[c] = car
            logit = z[c] - sp[c] - jnp.concatenate(after, axis=1)
            if diag:
                logit = jnp.where(causal[qq], logit, NEG_BIG)
            att.append(jnp.exp2(logit))
        for c in range(len(chains)):
            acc_ref[c] += _mm(att[c], vb)

    def sweep_pair(q0, n_wide, half_step):
        acc_ref[...] = jnp.zeros_like(acc_ref)
        car_ref[...] = jnp.zeros_like(car_ref)
        process(q0, q0, SB_QUERY_GROUP, True)
        base = q0
        if half_step:
            base = q0 - SB_QUERY_GROUP * blk
            process(q0, base, SB_QUERY_GROUP, False)
        wide = SB_KEY_GROUP * blk

        def kstep(jj, _):
            process(q0, pl.multiple_of(base - (jj + 1) * wide, wide), SB_KEY_GROUP, False)
            return 0

        lax.fori_loop(0, n_wide, kstep, 0)
        for qq in range(SB_QUERY_GROUP):
            o_ref[pl.ds(q0 + qq * blk, blk), :] = jnp.where(m0, acc_ref[2 * qq], acc_ref[2 * qq + 1]
                                                            ).astype(o_ref.dtype)

    quad = SB_KEY_GROUP * blk
    n_quads = n_qblocks // SB_KEY_GROUP

    def quad_step(m, _):
        q0 = pl.multiple_of(m * quad, quad)
        sweep_pair(q0, m, False)
        sweep_pair(q0 + SB_QUERY_GROUP * blk, m, True)
        return 0

    lax.fori_loop(0, n_quads, quad_step, 0)
    if n_qblocks % SB_KEY_GROUP:
        sweep_pair(n_quads * quad, n_quads, False)


def _stick_breaking(q, k, v, bsz, t_len):
    n, bw = q.shape
    n_pairs = bw // V7X_LANES
    blk = SB_BLOCK
    idx = np.arange(blk)
    su = np.concatenate([(idx[:, None] > idx[None, :]).astype(np.float32), np.ones((blk, blk), np.float32)], axis=1)
    su = jnp.asarray(su, BF16)
    tok = pl.BlockSpec((t_len, V7X_LANES), lambda b, p: (b, p))
    assert SB_KEY_GROUP == 2 * SB_QUERY_GROUP and t_len % (SB_QUERY_GROUP * blk) == 0
    qgroup = SB_QUERY_GROUP
    return pl.pallas_call(
        _sb_body,
        grid=(bsz, n_pairs),
        in_specs=[tok, tok, tok, pl.BlockSpec(su.shape, lambda b, p: (0, 0))],
        out_specs=tok,
        out_shape=jax.ShapeDtypeStruct((n, bw), BF16),
        scratch_shapes=[pltpu.VMEM((2 * qgroup, blk, V7X_LANES), F32),
                        pltpu.VMEM((2 * qgroup, blk, blk), F32)],
        compiler_params=_cparams(("parallel", "parallel")),
    )(q, k, v, su)


def _lru_body(gate_ref, rec_ref, cw_ref, pv_ref, wr_ref, wi_ref, o_ref, prev_ref, h_ref):
    tt = rec_ref.shape[0]
    width = rec_ref.shape[1]
    n_blocks = width // V7X_LANES

    @pl.when(pl.program_id(1) == 0)
    def _():
        prev_ref[...] = jnp.zeros_like(prev_ref)
        h_ref[...] = jnp.zeros_like(h_ref)

    row8 = lax.broadcasted_iota(jnp.int32, (V7X_SUBLANES, 1), 0)

    for gi in range(n_blocks):
        sl = slice(gi * V7X_LANES, (gi + 1) * V7X_LANES)
        u = rec_ref[:, sl].astype(F32)
        prev = prev_ref[:, sl]

        def shifted(s):
            ru = pltpu.roll(u, s, 0)
            first = jnp.where(row8 < s, pltpu.roll(prev, s, 0), ru[0:V7X_SUBLANES])
            return jnp.concatenate([first, ru[V7X_SUBLANES:]], axis=0)

        cw = cw_ref[:, sl]
        rc = (cw[3:4] * u + cw[2:3] * shifted(1) + cw[1:2] * shifted(2) + cw[0:1] * shifted(3)
              + pv_ref[0:1, sl])
        prev_ref[:, sl] = u[tt - V7X_SUBLANES:tt]

        rgate = _sigmoid(_mm(rc, wr_ref[gi]) + pv_ref[1:2, sl])
        igate = _sigmoid(_mm(rc, wi_ref[gi]) + pv_ref[2:3, sl])
        log_a = -LRU_C * rgate * _softplus(-pv_ref[3:4, sl])
        a = jnp.exp(log_a)
        om = 1.0 - a * a
        b = jnp.where(om > 0.0, om * lax.rsqrt(om), 0.0) * (igate * rc)

        s = 1
        while s < V7X_SUBLANES:
            ra = pltpu.roll(a, s, 0)
            rb = pltpu.roll(b, s, 0)
            a_sh = jnp.concatenate([jnp.where(row8 >= s, ra[0:V7X_SUBLANES], 1.0), ra[V7X_SUBLANES:]], axis=0)
            b_sh = jnp.concatenate([jnp.where(row8 >= s, rb[0:V7X_SUBLANES], 0.0), rb[V7X_SUBLANES:]], axis=0)
            b = a * b_sh + b
            a = a * a_sh
            s *= 2
        n_groups = tt // V7X_SUBLANES
        ag = [a[j * V7X_SUBLANES:(j + 1) * V7X_SUBLANES] for j in range(n_groups)]
        bg = [b[j * V7X_SUBLANES:(j + 1) * V7X_SUBLANES] for j in range(n_groups)]
        k = 1
        while k < n_groups:
            bg = [bg[j] if j < k else ag[j] * bg[j - k] + bg[j] for j in range(n_groups)]
            ag = [ag[j] if j < k else ag[j] * ag[j - k] for j in range(n_groups)]
            k *= 2
        h_in = h_ref[0:1, sl]
        h = jnp.concatenate([ag[j] * h_in + bg[j] for j in range(n_groups)], axis=0)
        h_ref[0:1, sl] = h[tt - 1:tt]
        o_ref[:, sl] = (jax.nn.gelu(gate_ref[:, sl].astype(F32)) * h).astype(o_ref.dtype)


def _rg_lru(gate, rec, cw, pv, wr, wi, bsz, t_len, tt):
    n, width = rec.shape
    nt = t_len // tt
    tok = pl.BlockSpec((tt, width), lambda b, t: (b * nt + t, 0))
    full = lambda arr: pl.BlockSpec(arr.shape, lambda b, t: (0,) * arr.ndim)
    return pl.pallas_call(
        _lru_body,
        grid=(bsz, nt),
        in_specs=[tok, tok, full(cw), full(pv), full(wr), full(wi)],
        out_specs=tok,
        out_shape=jax.ShapeDtypeStruct((n, width), BF16),
        scratch_shapes=[pltpu.VMEM((V7X_SUBLANES, width), F32), pltpu.VMEM((V7X_SUBLANES, width), F32)],
        compiler_params=_cparams(("parallel", "arbitrary")),
    )(gate, rec, cw, pv, wr, wi)


def _proj_norm_route_body(*refs, n_in):
    x_ref = refs[0]
    a_refs = refs[1:1 + n_in]
    w_refs = refs[1 + n_in:1 + 2 * n_in]
    g_ref, whl_ref, b_ref, slt_ref, xo_ref, h_ref, rt_ref, cnt_ref, run_ref = refs[1 + 2 * n_in:]

    @pl.when(pl.program_id(0) == 0)
    def _():
        run_ref[...] = jnp.zeros_like(run_ref)

    x = x_ref[...]
    for a_ref, w_ref in zip(a_refs, w_refs):
        x = x + jnp.dot(a_ref[...], w_ref[...], preferred_element_type=F32)
    xo_ref[...] = x
    ms = jnp.mean(x * x, axis=-1, keepdims=True)
    h = x * lax.rsqrt(ms + RMS_EPS) * g_ref[...]
    h_hi, h_lo = _split_hilo(h)
    h_ref[...] = h_hi
    both = jnp.dot(h_hi, whl_ref[...], preferred_element_type=F32)
    lg = (both[:, :V7X_LANES] + both[:, V7X_LANES:]
          + jnp.dot(h_lo, whl_ref[:, :V7X_LANES], preferred_element_type=F32)) + b_ref[0:1]
    lane = lax.broadcasted_iota(jnp.int32, lg.shape, 1)
    neg_inf = -jnp.inf
    is_g = lane < N_GROUPS
    gl = jnp.where(is_g, lg, neg_inf)
    gmax = jnp.max(gl, axis=-1, keepdims=True)
    gidx = jnp.min(jnp.where(gl == gmax, lane, V7X_LANES), axis=-1, keepdims=True)
    g_top = 1.0 / jnp.sum(jnp.where(is_g, jnp.exp(gl - gmax), 0.0), axis=-1, keepdims=True)
    lo_e = N_GROUPS + EXPERTS_PER_GROUP * gidx
    el = jnp.where((lane >= lo_e) & (lane < lo_e + EXPERTS_PER_GROUP), lg, neg_inf)
    m1 = jnp.max(el, axis=-1, keepdims=True)
    i1 = jnp.min(jnp.where(el == m1, lane, V7X_LANES), axis=-1, keepdims=True)
    el2 = jnp.where(lane == i1, neg_inf, el)
    m2 = jnp.max(el2, axis=-1, keepdims=True)
    i2 = jnp.min(jnp.where(el2 == m2, lane, V7X_LANES), axis=-1, keepdims=True)
    t = jnp.exp(m2 - m1)
    gate1 = g_top / (1.0 + t)
    gate2 = g_top * t / (1.0 + t)
    oh1 = lane == i1
    oh2 = lane == i2
    oh = jnp.where(oh1 | oh2, 1.0, 0.0)
    before = jnp.dot(slt_ref[...], oh.astype(BF16), preferred_element_type=F32) + run_ref[0:1]
    rank1 = jnp.sum(jnp.where(oh1, before, 0.0), axis=-1, keepdims=True)
    rank2 = jnp.sum(jnp.where(oh2, before, 0.0), axis=-1, keepdims=True)
    run_ref[0:1] = run_ref[0:1] + jnp.sum(oh, axis=0, keepdims=True)
    cnt_ref[...] = jnp.broadcast_to(run_ref[0:1], cnt_ref.shape)
    vals = (i1 - N_GROUPS, i2 - N_GROUPS, rank1, rank2, gate1, gate2)
    out = jnp.zeros(lg.shape, F32)
    for li, val in enumerate(vals):
        out = jnp.where(lane == li, val.astype(F32), out)
    rt_ref[...] = out.T[0:V7X_SUBLANES]


def _proj_norm_route(x, acts, ws, g, w_router, b_router, tm):
    n, d = x.shape
    n_in = len(acts)
    nr = w_router.shape[1]
    wpad = jnp.zeros((d, V7X_LANES), F32).at[:, :nr].set(w_router)
    wh = wpad.astype(BF16)
    whl = jnp.concatenate([wh, (wpad - wh.astype(F32)).astype(BF16)], axis=1)
    bias = jnp.zeros((V7X_SUBLANES, V7X_LANES), F32).at[0, :nr].set(b_router)
    slt = jnp.asarray(np.tril(np.ones((tm, tm), np.float32), -1), BF16)
    const = lambda shape: pl.BlockSpec(shape, lambda i: (0, 0))
    rows = lambda width: pl.BlockSpec((tm, width), lambda i: (i, 0))
    return pl.pallas_call(
        functools.partial(_proj_norm_route_body, n_in=n_in),
        grid=(n // tm,),
        in_specs=([rows(d)] + [rows(a.shape[1]) for a in acts] + [const(w.shape) for w in ws]
                  + [const((1, d)), const((d, 2 * V7X_LANES)),
                     const((V7X_SUBLANES, V7X_LANES)), const((tm, tm))]),
        out_specs=[rows(d), rows(d), pl.BlockSpec((V7X_SUBLANES, tm), lambda i: (0, i)),
                   const((V7X_SUBLANES, V7X_LANES))],
        out_shape=[jax.ShapeDtypeStruct((n, d), F32), jax.ShapeDtypeStruct((n, d), BF16),
                   jax.ShapeDtypeStruct((V7X_SUBLANES, n), F32),
                   jax.ShapeDtypeStruct((V7X_SUBLANES, V7X_LANES), F32)],
        scratch_shapes=[pltpu.VMEM((V7X_SUBLANES, V7X_LANES), F32)],
        compiler_params=_cparams(("arbitrary",)),
    )(x, *acts, *ws, g, whl, bias, slt)


def _moe_body(te_ref, nv_ref, x_ref, wg_ref, wu_ref, wd_ref, o_ref, wgb_ref, wub_ref, wdb_ref):
    i = pl.program_id(0)
    valid = i < nv_ref[0]

    @pl.when(valid & ((i == 0) | (te_ref[i] != te_ref[jnp.maximum(i - 1, 0)])))
    def _():
        wgb_ref[...] = wg_ref[0].astype(BF16)
        wub_ref[...] = wu_ref[0].astype(BF16)
        wdb_ref[...] = wd_ref[0].astype(BF16)

    @pl.when(valid)
    def _():
        x = x_ref[...]
        gt = jnp.dot(x, wgb_ref[...], preferred_element_type=F32)
        up = jnp.dot(x, wub_ref[...], preferred_element_type=F32)
        hid = (gt * _sigmoid(gt)) * up
        o_ref[...] = jnp.dot(hid.astype(BF16), wdb_ref[...], preferred_element_type=F32).astype(o_ref.dtype)


def _moe_grouped(xs, tile_expert, n_valid, wg, wu, wd, layer):
    p_rows, d = xs.shape
    de = wg.shape[3]
    tm = MOE_TILE
    grid_spec = pltpu.PrefetchScalarGridSpec(
        num_scalar_prefetch=2,
        grid=(p_rows // tm,),
        in_specs=[pl.BlockSpec((tm, d), lambda i, te, nv: (i, 0)),
                  pl.BlockSpec((None, 1, d, de), lambda i, te, nv: (layer, te[i], 0, 0)),
                  pl.BlockSpec((None, 1, d, de), lambda i, te, nv: (layer, te[i], 0, 0)),
                  pl.BlockSpec((None, 1, de, d), lambda i, te, nv: (layer, te[i], 0, 0))],
        out_specs=pl.BlockSpec((tm, d), lambda i, te, nv: (i, 0)),
        scratch_shapes=[pltpu.VMEM((d, de), BF16), pltpu.VMEM((d, de), BF16), pltpu.VMEM((de, d), BF16)],
    )
    return pl.pallas_call(
        _moe_body,
        grid_spec=grid_spec,
        out_shape=jax.ShapeDtypeStruct((p_rows, d), BF16),
        compiler_params=_cparams(("arbitrary",)),
    )(tile_expert, n_valid, xs, wg, wu, wd)


def _mix_out_moe(x, acts, ws, g_norm, w_group, b_group, w_erouter, b_erouter, wg, wu, wd, layer, tm):
    n, d = x.shape
    x, hn, route, cnt = _proj_norm_route(x, acts, ws, g_norm, jnp.concatenate([w_group, w_erouter], axis=1),
                                         jnp.concatenate([b_group, b_erouter]), tm)
    expert_id = route[0:2].astype(jnp.int32)
    rank = route[2:4].astype(jnp.int32)
    gates = route[4:6]
    counts = cnt[0, N_GROUPS:N_GROUPS + N_EXPERTS].astype(jnp.int32)

    tmm = MOE_TILE
    padded = ((counts + tmm - 1) // tmm) * tmm
    ends = jnp.cumsum(padded)
    starts = ends - padded
    start_of = jnp.zeros_like(expert_id)
    for e in range(N_EXPERTS):
        start_of = jnp.where(expert_id == e, starts[e], start_of)
    pos = start_of + rank
    p_rows = 2 * n + N_EXPERTS * tmm
    n_tiles = p_rows // tmm
    tok = jnp.broadcast_to(jnp.arange(n, dtype=jnp.int32)[None, :], (2, n))
    row_token = (jnp.arange(p_rows, dtype=jnp.int32) % n).at[pos.reshape(-1)].set(
        tok.reshape(-1), mode="promise_in_bounds", unique_indices=True)
    tile_expert = jnp.minimum(
        jnp.searchsorted(ends, jnp.arange(n_tiles, dtype=jnp.int32) * tmm, side="right"), N_EXPERTS - 1
    ).astype(jnp.int32)
    n_valid = (ends[-1] // tmm).astype(jnp.int32).reshape(1)

    take_rows = lambda a, idx: jnp.take(a, idx, axis=0, mode="clip")
    xs = take_rows(hn, row_token)
    ys = _moe_grouped(xs, tile_expert, n_valid, wg, wu, wd, layer)
    y = (gates[0][:, None] * take_rows(ys, pos[0]).astype(F32)
         + gates[1][:, None] * take_rows(ys, pos[1]).astype(F32))
    return x + y


def _pad_rows(w, rows, offset=0):
    out = jnp.zeros((rows, w.shape[1]), w.dtype)
    return out.at[offset:offset + w.shape[0]].set(w)


def kernel(x, norm_mix, norm_ffn, w_in_even, mu_a, w0, decay_up, a0, iclr_up, gate_up, k_k, k_a, r_k, gn_w,
           gn_b, q_norm_g, k_norm_g, w_out_even, w_in_odd, conv_w, conv_b, w_rgate, b_rgate, w_igate,
           b_igate, lru_lambda, w_out_odd, w_group, b_group, w_erouter, b_erouter, exp_w_gate, exp_w_up,
           exp_w_down):
    bsz, t_len, d = x.shape
    n = bsz * t_len
    depth = norm_mix.shape[0]
    aw = decay_up.shape[2]
    d_lora, i_lora, g_lora = decay_up.shape[1], iclr_up.shape[1], gate_up.shape[1]
    bw = (w_in_even.shape[2] - (3 * aw + d_lora + i_lora + g_lora)) // 3
    lw = w_rgate.shape[1] * w_rgate.shape[2]
    assert d_lora + i_lora == V7X_LANES and g_lora <= 2 * V7X_LANES
    lora_w = 3 * V7X_LANES
    tm = 512
    xf = x.reshape(n, d)

    for layer in range(depth):
        gm = norm_mix[layer].reshape(1, d)
        if layer % 2 == 0:
            i = layer // 2
            w_in = w_in_even[i]
            o3 = 3 * aw
            a_cols = o3 + d_lora + i_lora + g_lora
            pad = jnp.zeros((d, lora_w - (d_lora + i_lora + g_lora)), F32)
            w_cat = jnp.concatenate([w_in[:, :a_cols], pad, w_in[:, a_cols:]], axis=1).astype(BF16)
            n_rep = bw // HEAD_DIM
            head_gains = jnp.zeros((V7X_SUBLANES, bw), F32).at[0].set(
                jnp.tile(q_norm_g[i], n_rep) * (LOG2_E / math.sqrt(HEAD_DIM))).at[1].set(jnp.tile(k_norm_g[i], n_rep))
            r, k, v, lo, qb, kb, vb = _norm_matmul(xf, gm, w_cat, (aw, aw, aw, lora_w, bw, bw, bw), tm,
                                                   head_norm={4: 0, 5: 1}, head_gains=head_gains)
            mu = mu_a[i]
            rows = [mu[:aw], mu[aw:2 * aw], mu[2 * aw:o3], w0[i], a0[i], k_k[i], k_a[i], r_k[i].reshape(-1),
                    gn_w[i], gn_b[i]]
            prm = jnp.zeros((16, aw), F32).at[:len(rows)].set(jnp.stack(rows))
            mul = jnp.zeros((V7X_SUBLANES, lora_w), F32).at[0, :a_cols - o3].set(mu[o3:])
            du = _pad_rows(decay_up[i], V7X_LANES, 0).astype(BF16)
            iu = _pad_rows(iclr_up[i], V7X_LANES, d_lora).astype(BF16)
            gu = _pad_rows(gate_up[i], 2 * V7X_LANES, 0).astype(BF16)
            ya = _rwkv(r, k, v, lo, prm, mul, du, iu, gu, bsz, t_len)
            yb = _stick_breaking(qb, kb, vb, bsz, t_len)
            w_out = w_out_even[i].astype(BF16)
            acts, ws = [ya, yb], [w_out[:aw], w_out[aw:]]
        else:
            j = layer // 2
            gate, rec = _norm_matmul(xf, gm, w_in_odd[j].astype(BF16), (lw, lw), tm)
            cw = jnp.zeros((V7X_SUBLANES, lw), F32).at[:conv_w.shape[1]].set(conv_w[j])
            pv = jnp.zeros((V7X_SUBLANES, lw), F32).at[:4].set(
                jnp.stack([conv_b[j], b_rgate[j], b_igate[j], lru_lambda[j]]))
            yl = _rg_lru(gate, rec, cw, pv, w_rgate[j].astype(BF16), w_igate[j].astype(BF16), bsz, t_len,
                         min(256, t_len))
            acts, ws = [yl], [w_out_odd[j].astype(BF16)]
        xf = _mix_out_moe(xf, acts, ws, norm_ffn[layer].reshape(1, d), w_group[layer], b_group[layer],
                          w_erouter[layer], b_erouter[layer], exp_w_gate, exp_w_up, exp_w_down, layer, tm)
    return xf.reshape(bsz, t_len, d)
```

```python
import functools
import math

import numpy as np
import jax
import jax.numpy as jnp
from jax import lax
from jax.experimental import pallas as pl
from jax.experimental.pallas import tpu as pltpu

F32 = jnp.float32
BF16 = jnp.bfloat16

V7X_LANES = 128
V7X_SUBLANES = 8
V7X_VMEM_LIMIT_BYTES = 48 * 1024 * 1024

HEAD_DIM = 64
RMS_EPS = 1e-6
GN_EPS = 64e-5
LRU_C = 8.0
N_GROUPS = 4
EXPERTS_PER_GROUP = 4
N_EXPERTS = 16
RWKV_CHUNK = 128
RWKV_CHUNKS_PER_STEP = 2
SB_BLOCK = 128
SB_KEY_GROUP = 4
SB_QUERY_GROUP = 2
MOE_TILE = 512
NEG_BIG = -1e30
LOG2_E = 1.4426950408889634


def _cparams(sem):
    return pltpu.CompilerParams(dimension_semantics=sem, vmem_limit_bytes=V7X_VMEM_LIMIT_BYTES)


def _mm(a, b):
    return jnp.dot(a.astype(BF16), b.astype(BF16), preferred_element_type=F32)


def _mm_nt(a, b):
    return lax.dot_general(a.astype(BF16), b.astype(BF16), (((1,), (1,)), ((), ())),
                           preferred_element_type=F32)


def _split_hilo(x):
    hi = x.astype(BF16)
    lo = (x - hi.astype(F32)).astype(BF16)
    return hi, lo


def _mm_hilo(x, m):
    hi, lo = _split_hilo(x)
    return (jnp.dot(hi, m, preferred_element_type=F32) + jnp.dot(lo, m, preferred_element_type=F32))


def _hilo_mm(m, x):
    hi, lo = _split_hilo(x)
    return (jnp.dot(m, hi, preferred_element_type=F32) + jnp.dot(m, lo, preferred_element_type=F32))


def _sigmoid(x):
    return 0.5 * jnp.tanh(0.5 * x) + 0.5


def _neg_abs(x):
    bits = pltpu.bitcast(x, jnp.uint32) | jnp.uint32(0x80000000)
    return pltpu.bitcast(bits, F32)


def _softplus(x):
    return jnp.maximum(x, 0.0) + jnp.log(1.0 + jnp.exp(-jnp.abs(x)))


def _norm_matmul_body(x_ref, g_ref, w_ref, *rest, col_splits, head_norm):
    if head_norm:
        bones_ref, hg_ref = rest[:2]
        out_refs = rest[2:]
    else:
        out_refs = rest
    x = x_ref[...]
    ms = jnp.mean(x * x, axis=-1, keepdims=True)
    h = (x * lax.rsqrt(ms + RMS_EPS) * g_ref[...]).astype(BF16)
    off = 0
    for gi, (o_ref, c) in enumerate(zip(out_refs, col_splits)):
        y = jnp.dot(h, w_ref[:, off:off + c], preferred_element_type=F32)
        if gi in head_norm:
            row = head_norm[gi]
            msq = jnp.dot((y * y).astype(BF16), bones_ref[...], preferred_element_type=F32) * (1.0 / HEAD_DIM)
            y = y * lax.rsqrt(msq + RMS_EPS) * hg_ref[row:row + 1]
        o_ref[...] = y.astype(o_ref.dtype)
        off += c


def _norm_matmul(x, g, w, col_splits, tm, head_norm=None, head_gains=None):
    n, d = x.shape
    ctot = w.shape[1]
    assert sum(col_splits) == ctot and n % tm == 0
    head_norm = dict(head_norm or {})
    extra, extra_specs = [], []
    if head_norm:
        width = col_splits[next(iter(head_norm))]
        assert all(col_splits[gi] == width for gi in head_norm)
        heads = np.arange(width) // HEAD_DIM
        extra = [jnp.asarray((heads[:, None] == heads[None, :]).astype(np.float32), BF16), head_gains]
        extra_specs = [pl.BlockSpec((width, width), lambda i: (0, 0)),
                       pl.BlockSpec(head_gains.shape, lambda i: (0, 0))]
    return pl.pallas_call(
        functools.partial(_norm_matmul_body, col_splits=tuple(col_splits), head_norm=head_norm),
        grid=(n // tm,),
        in_specs=[pl.BlockSpec((tm, d), lambda i: (i, 0)),
                  pl.BlockSpec((1, d), lambda i: (0, 0)),
                  pl.BlockSpec((d, ctot), lambda i: (0, 0))] + extra_specs,
        out_specs=[pl.BlockSpec((tm, c), lambda i: (i, 0)) for c in col_splits],
        out_shape=[jax.ShapeDtypeStruct((n, c), BF16) for c in col_splits],
        compiler_params=_cparams(("parallel",)),
    )(x, g, w, *extra)


def _tri_inverse_many(mats, row, col):
    eye = (row == col).astype(F32)
    blk8 = (row // 8) == (col // 8)
    n1 = [jnp.where(blk8, a, 0.0) for a in mats]
    n2 = [_mm(n, n) for n in n1]
    ipn = [eye + n for n in n1]
    p = [i + _mm(i, m) for i, m in zip(ipn, n2)]
    n4 = [_mm(m, m) for m in n2]
    x = [q + _mm(q, m) for q, m in zip(p, n4)]
    k = 8
    c = mats[0].shape[0]
    while k < c:
        sel = ((row // (2 * k)) == (col // (2 * k))) & ((row // k) > (col // k))
        e = [jnp.where(sel, a, 0.0) for a in mats]
        xe = [_mm(xi, ei) for xi, ei in zip(x, e)]
        x = [xi + _mm(xei, xi) for xi, xei in zip(x, xe)]
        k *= 2
    return x


def _rwkv_body(r_ref, k_ref, v_ref, l_ref, p_ref, mul_ref, du_ref, iu_ref, gu_ref, bones_ref, lt_ref,
               o_ref, z_ref, carry_ref):
    c = RWKV_CHUNK
    rows = r_ref.shape[0]
    n_chunks = rows // c
    aw = r_ref.shape[1]
    lw = l_ref.shape[1]
    n_pairs = aw // V7X_LANES

    @pl.when(pl.program_id(1) == 0)
    def _():
        z_ref[...] = jnp.zeros_like(z_ref)
        carry_ref[...] = jnp.zeros_like(carry_ref)

    row1 = lax.broadcasted_iota(jnp.int32, (rows, 1), 0)

    def token_shift(raw, prev_last, mu):
        prev = jnp.where(row1 == 0, prev_last, pltpu.roll(raw, 1, 0))
        return raw + mu * (prev - raw)

    prm = p_ref[...]
    mu_r, mu_k, mu_v, w0, a0, k_k, k_a, r_k, gn_w, gn_b = (prm[i:i + 1] for i in range(10))

    r_raw = r_ref[...].astype(F32)
    k_raw = k_ref[...].astype(F32)
    v_raw = v_ref[...].astype(F32)
    l_raw = l_ref[...].astype(F32)
    r = token_shift(r_raw, carry_ref[0:1, 0:aw], mu_r)
    k = token_shift(k_raw, carry_ref[0:1, aw:2 * aw], mu_k)
    v = token_shift(v_raw, carry_ref[0:1, 2 * aw:3 * aw], mu_v)
    xl = token_shift(l_raw, carry_ref[0:1, 3 * aw:3 * aw + lw], mul_ref[0:1])
    carry_ref[0:1, 0:aw] = r_raw[rows - 1:rows]
    carry_ref[0:1, aw:2 * aw] = k_raw[rows - 1:rows]
    carry_ref[0:1, 2 * aw:3 * aw] = v_raw[rows - 1:rows]
    carry_ref[0:1, 3 * aw:3 * aw + lw] = l_raw[rows - 1:rows]

    l0 = xl[:, :V7X_LANES]
    l1 = xl[:, V7X_LANES:]
    dw = _mm(jnp.tanh(l0), du_ref[...])
    da = _mm(l0, iu_ref[...])
    g = _mm(_sigmoid(l1), gu_ref[...])
    w_log = -_softplus(-(w0 + dw)) - 0.5
    logw = -jnp.exp(w_log)
    a = _sigmoid(a0 + da)

    bones = bones_ref[...]

    def head_sums(t):
        return jnp.concatenate([_mm(t[:, p * V7X_LANES:(p + 1) * V7X_LANES], bones)
                                for p in range(n_pairs)], axis=1)

    kk = k * k_k
    ssq = head_sums(kk * kk)
    kk = kk * jnp.minimum(lax.rsqrt(ssq), 1e12)
    k_mod = k * (1.0 + (a - 1.0) * k_a)
    a_vec = -kk
    b_vec = kk * a

    cum_all = _hilo_mm(lt_ref[...], logw)

    row = lax.broadcasted_iota(jnp.int32, (c, c), 0)
    col = lax.broadcasted_iota(jnp.int32, (c, c), 1)
    strict = col < row
    incl = col <= row
    eye = row == col
    same_head = (row // HEAD_DIM) == (col // HEAD_DIM)
    m0 = lax.broadcasted_iota(jnp.int32, (1, V7X_LANES), 1) < HEAD_DIM

    def stack(x):
        return jnp.concatenate([jnp.where(m0, x, 0.0), jnp.where(m0, 0.0, x)], axis=0)

    a_s, a_t, r_s, r_t, b_s, k_s, b_p, k_p, p_end, v_c = ([] for _ in range(10))
    for j in range(n_chunks):
        rs = slice(j * c, (j + 1) * c)
        cum = cum_all[rs]
        cmid = cum[c // 2 - 1:c // 2]
        cend = cum[c - 1:c]
        emid = jnp.exp(cmid)
        a_s.append(a_vec[rs] * jnp.exp(cum - logw[rs] - cmid))
        a_t.append(a_s[j] * emid)
        r_s.append(r[rs] * jnp.exp(cum - cmid))
        r_t.append(r_s[j] * emid)
        e_neg = jnp.exp(cmid - cum)
        b_s.append(b_vec[rs] * e_neg)
        k_s.append(k_mod[rs] * e_neg)
        e_end = jnp.exp(cend - cum)
        b_p.append(b_vec[rs] * e_end)
        k_p.append(k_mod[rs] * e_end)
        p_end.append(jnp.exp(cend))
        v_c.append(v[rs])

    sls = [slice(p * V7X_LANES, (p + 1) * V7X_LANES) for p in range(n_pairs)]
    units = [(j, p) for j in range(n_chunks) for p in range(n_pairs)]
    nu = range(len(units))
    v_p = [v_c[j][:, sls[p]] for j, p in units]
    gm = [_mm_nt(jnp.concatenate([stack(a_s[j][:, sls[p]]), stack(r_s[j][:, sls[p]])], axis=0),
                 jnp.concatenate([b_s[j][:, sls[p]], k_s[j][:, sls[p]]], axis=0)) for j, p in units]
    aab = [jnp.where(strict, gm[u][h * c:(h + 1) * c, 0:c], 0.0) for u in nu for h in range(2)]
    tinv = _tri_inverse_many(aab, row, col)
    aak = [jnp.concatenate([jnp.where(strict, gm[u][h * c:(h + 1) * c, c:2 * c], 0.0) for h in range(2)],
                           axis=1) for u in nu]
    arb = [jnp.concatenate([jnp.where(incl, gm[u][(2 + h) * c:(3 + h) * c, 0:c], 0.0) for h in range(2)],
                           axis=1) for u in nu]
    ark = [jnp.concatenate([jnp.where(incl, gm[u][(2 + h) * c:(3 + h) * c, c:2 * c], 0.0) for h in range(2)],
                           axis=1) for u in nu]
    vs = [stack(x) for x in v_p]
    akv = [_mm(aak[u], vs[u]) for u in nu]
    wu = [_mm(jnp.concatenate([tinv[2 * u], tinv[2 * u + 1]], axis=1),
              jnp.concatenate([stack(a_t[j][:, sls[p]]), stack(akv[u])], axis=1))
          for u, (j, p) in enumerate(units)]
    w_t = [x[:, :V7X_LANES] for x in wu]
    u_v = [x[:, V7X_LANES:] for x in wu]
    q_t = [r_t[j][:, sls[p]] + _mm(arb[u], stack(w_t[u])) for u, (j, p) in enumerate(units)]
    y_v = [_mm(jnp.concatenate([arb[u], ark[u]], axis=1),
               jnp.concatenate([stack(u_v[u]), vs[u]], axis=0)) for u in nu]
    bp_t = [b_p[j][:, sls[p]].T for j, p in units]
    kp_t = [k_p[j][:, sls[p]].T for j, p in units]
    m_mat = [jnp.where(same_head, _mm(bp_t[u], w_t[u]), 0.0) for u in nu]
    n_mat = [jnp.where(same_head, _mm(jnp.concatenate([bp_t[u], kp_t[u]], axis=1),
                                      jnp.concatenate([u_v[u], v_p[u]], axis=0)), 0.0) for u in nu]
    p_col = [jnp.sum(jnp.where(eye, p_end[j][:, sls[p]], 0.0), axis=1, keepdims=True) for j, p in units]
    z = [z_ref[p] for p in range(n_pairs)]
    y_rows = []
    for j in range(n_chunks):
        y_parts = []
        for p in range(n_pairs):
            u = j * n_pairs + p
            y_parts.append(_mm(q_t[u], z[p]) + y_v[u])
            z[p] = p_col[u] * z[p] + _mm(m_mat[u], z[p]) + n_mat[u]
        y_rows.append(jnp.concatenate(y_parts, axis=1))
    for p in range(n_pairs):
        z_ref[p] = z[p]

    y = jnp.concatenate(y_rows, axis=0)
    inv_hd = 1.0 / HEAD_DIM
    mean = head_sums(y) * inv_hd
    d = y - mean
    var = head_sums(d * d) * inv_hd
    yn = d * lax.rsqrt(var + GN_EPS) * gn_w + gn_b
    bonus = head_sums(r * k_mod * r_k) * v
    o_ref[...] = ((yn + bonus) * g).astype(o_ref.dtype)


def _rwkv(r, k, v, l, prm, mul, du, iu, gu, bsz, t_len):
    n, aw = r.shape
    lw = l.shape[1]
    c = RWKV_CHUNK
    n_chunks = math.gcd(RWKV_CHUNKS_PER_STEP, t_len // c)
    rows = n_chunks * c
    nt = t_len // rows
    heads = np.arange(V7X_LANES) // HEAD_DIM
    bones = jnp.asarray((heads[:, None] == heads[None, :]).astype(np.float32), BF16)
    lt = jnp.asarray(np.kron(np.eye(n_chunks, dtype=np.float32), np.tril(np.ones((c, c), np.float32))), BF16)
    tok = lambda w: pl.BlockSpec((rows, w), lambda b, t: (b * nt + t, 0))
    full = lambda arr: pl.BlockSpec(arr.shape, lambda b, t: (0,) * arr.ndim)
    return pl.pallas_call(
        _rwkv_body,
        grid=(bsz, nt),
        in_specs=[tok(aw), tok(aw), tok(aw), tok(lw), full(prm), full(mul), full(du), full(iu), full(gu),
                  full(bones), full(lt)],
        out_specs=tok(aw),
        out_shape=jax.ShapeDtypeStruct((n, aw), BF16),
        scratch_shapes=[pltpu.VMEM((aw // V7X_LANES, V7X_LANES, V7X_LANES), F32),
                        pltpu.VMEM((V7X_SUBLANES, 3 * aw + lw), F32)],
        compiler_params=_cparams(("parallel", "arbitrary")),
    )(r, k, v, l, prm, mul, du, iu, gu, bones, lt)


def _sb_body(q_ref, k_ref, v_ref, su_ref, o_ref, acc_ref, car_ref):
    t_len = q_ref.shape[0]
    blk = SB_BLOCK
    m0 = lax.broadcasted_iota(jnp.int32, (1, V7X_LANES), 1) < HEAD_DIM
    head_mask = (m0, jnp.logical_not(m0))

    n_qblocks = t_len // blk
    chains = [(qq, h) for qq in range(SB_QUERY_GROUP) for h in range(2)]

    def process(q0, k0, n_kb, diag):
        width = n_kb * blk
        kb = k_ref[pl.ds(k0, width), :]
        vb = v_ref[pl.ds(k0, width), :]
        qb = [q_ref[pl.ds(q0 + qq * blk, blk), :] for qq in range(SB_QUERY_GROUP)]
        z = [_mm_nt(jnp.where(head_mask[h], qb[qq], jnp.zeros_like(qb[qq])), kb) for qq, h in chains]
        sp = [jnp.maximum(zc, 0.0) + jnp.log2(1.0 + jnp.exp2(_neg_abs(zc))) for zc in z]
        if diag:
            row = lax.broadcasted_iota(jnp.int32, (blk, width), 0)
            col = lax.broadcasted_iota(jnp.int32, (blk, width), 1)
            causal = [(col + k0) < (row + q0 + qq * blk) for qq in range(SB_QUERY_GROUP)]
            sp = [jnp.where(causal[qq], x, 0.0) for (qq, _), x in zip(chains, sp)]
        sp_lp = [x.astype(BF16) for x in sp]
        su = su_ref[...]
        rs = [[jnp.dot(x[:, b * blk:(b + 1) * blk], su, preferred_element_type=F32) for b in range(n_kb)]
              for x in sp_lp]
        att = []
        for c, (qq, _) in enumerate(chains):
            car = car_ref[c]
            after = [None] * n_kb
            for b in reversed(range(n_kb)):
                after[b] = rs[c][b][:, :blk] + car
                car = car + rs[c][b][:, blk:]
            car_ref[c] = car
            logit = z[c] - sp[c] - jnp.concatenate(after, axis=1)
            if diag:
                logit = jnp.where(causal[qq], logit, NEG_BIG)
            att.append(jnp.exp2(logit))
        for c in range(len(chains)):
            acc_ref[c] += _mm(att[c], vb)

    def sweep_pair(q0, n_wide, half_step):
        acc_ref[...] = jnp.zeros_like(acc_ref)
        car_ref[...] = jnp.zeros_like(car_ref)
        process(q0, q0, SB_QUERY_GROUP, True)
        base = q0
        if half_step:
            base = q0 - SB_QUERY_GROUP * blk
            process(q0, base, SB_QUERY_GROUP, False)
        wide = SB_KEY_GROUP * blk

        def kstep(jj, _):
            process(q0, pl.multiple_of(base - (jj + 1) * wide, wide), SB_KEY_GROUP, False)
            return 0

        lax.fori_loop(0, n_wide, kstep, 0)
        for qq in range(SB_QUERY_GROUP):
            o_ref[pl.ds(q0 + qq * blk, blk), :] = jnp.where(m0, acc_ref[2 * qq], acc_ref[2 * qq + 1]
                                                            ).astype(o_ref.dtype)

    quad = SB_KEY_GROUP * blk
    n_quads = n_qblocks // SB_KEY_GROUP

    def quad_step(m, _):
        q0 = pl.multiple_of(m * quad, quad)
        sweep_pair(q0, m, False)
        sweep_pair(q0 + SB_QUERY_GROUP * blk, m, True)
        return 0

    lax.fori_loop(0, n_quads, quad_step, 0)
    if n_qblocks % SB_KEY_GROUP:
        sweep_pair(n_quads * quad, n_quads, False)


def _stick_breaking(q, k, v, bsz, t_len):
    n, bw = q.shape
    n_pairs = bw // V7X_LANES
    blk = SB_BLOCK
    idx = np.arange(blk)
    su = np.concatenate([(idx[:, None] > idx[None, :]).astype(np.float32), np.ones((blk, blk), np.float32)], axis=1)
    su = jnp.asarray(su, BF16)
    tok = pl.BlockSpec((t_len, V7X_LANES), lambda b, p: (b, p))
    assert SB_KEY_GROUP == 2 * SB_QUERY_GROUP and t_len % (SB_QUERY_GROUP * blk) == 0
    qgroup = SB_QUERY_GROUP
    return pl.pallas_call(
        _sb_body,
        grid=(bsz, n_pairs),
        in_specs=[tok, tok, tok, pl.BlockSpec(su.shape, lambda b, p: (0, 0))],
        out_specs=tok,
        out_shape=jax.ShapeDtypeStruct((n, bw), BF16),
        scratch_shapes=[pltpu.VMEM((2 * qgroup, blk, V7X_LANES), F32),
                        pltpu.VMEM((2 * qgroup, blk, blk), F32)],
        compiler_params=_cparams(("parallel", "parallel")),
    )(q, k, v, su)


def _lru_body(gate_ref, rec_ref, cw_ref, pv_ref, wr_ref, wi_ref, o_ref, prev_ref, h_ref):
    tt = rec_ref.shape[0]
    width = rec_ref.shape[1]
    n_blocks = width // V7X_LANES

    @pl.when(pl.program_id(1) == 0)
    def _():
        prev_ref[...] = jnp.zeros_like(prev_ref)
        h_ref[...] = jnp.zeros_like(h_ref)

    row8 = lax.broadcasted_iota(jnp.int32, (V7X_SUBLANES, 1), 0)

    for gi in range(n_blocks):
        sl = slice(gi * V7X_LANES, (gi + 1) * V7X_LANES)
        u = rec_ref[:, sl].astype(F32)
        prev = prev_ref[:, sl]

        def shifted(s):
            ru = pltpu.roll(u, s, 0)
            first = jnp.where(row8 < s, pltpu.roll(prev, s, 0), ru[0:V7X_SUBLANES])
            return jnp.concatenate([first, ru[V7X_SUBLANES:]], axis=0)

        cw = cw_ref[:, sl]
        rc = (cw[3:4] * u + cw[2:3] * shifted(1) + cw[1:2] * shifted(2) + cw[0:1] * shifted(3)
              + pv_ref[0:1, sl])
        prev_ref[:, sl] = u[tt - V7X_SUBLANES:tt]

        rgate = _sigmoid(_mm(rc, wr_ref[gi]) + pv_ref[1:2, sl])
        igate = _sigmoid(_mm(rc, wi_ref[gi]) + pv_ref[2:3, sl])
        log_a = -LRU_C * rgate * _softplus(-pv_ref[3:4, sl])
        a = jnp.exp(log_a)
        om = 1.0 - a * a
        b = jnp.where(om > 0.0, om * lax.rsqrt(om), 0.0) * (igate * rc)

        s = 1
        while s < V7X_SUBLANES:
            ra = pltpu.roll(a, s, 0)
            rb = pltpu.roll(b, s, 0)
            a_sh = jnp.concatenate([jnp.where(row8 >= s, ra[0:V7X_SUBLANES], 1.0), ra[V7X_SUBLANES:]], axis=0)
            b_sh = jnp.concatenate([jnp.where(row8 >= s, rb[0:V7X_SUBLANES], 0.0), rb[V7X_SUBLANES:]], axis=0)
            b = a * b_sh + b
            a = a * a_sh
            s *= 2
        n_groups = tt // V7X_SUBLANES
        ag = [a[j * V7X_SUBLANES:(j + 1) * V7X_SUBLANES] for j in range(n_groups)]
        bg = [b[j * V7X_SUBLANES:(j + 1) * V7X_SUBLANES] for j in range(n_groups)]
        k = 1
        while k < n_groups:
            bg = [bg[j] if j < k else ag[j] * bg[j - k] + bg[j] for j in range(n_groups)]
            ag = [ag[j] if j < k else ag[j] * ag[j - k] for j in range(n_groups)]
            k *= 2
        h_in = h_ref[0:1, sl]
        h = jnp.concatenate([ag[j] * h_in + bg[j] for j in range(n_groups)], axis=0)
        h_ref[0:1, sl] = h[tt - 1:tt]
        o_ref[:, sl] = (jax.nn.gelu(gate_ref[:, sl].astype(F32)) * h).astype(o_ref.dtype)


def _rg_lru(gate, rec, cw, pv, wr, wi, bsz, t_len, tt):
    n, width = rec.shape
    nt = t_len // tt
    tok = pl.BlockSpec((tt, width), lambda b, t: (b * nt + t, 0))
    full = lambda arr: pl.BlockSpec(arr.shape, lambda b, t: (0,) * arr.ndim)
    return pl.pallas_call(
        _lru_body,
        grid=(bsz, nt),
        in_specs=[tok, tok, full(cw), full(pv), full(wr), full(wi)],
        out_specs=tok,
        out_shape=jax.ShapeDtypeStruct((n, width), BF16),
        scratch_shapes=[pltpu.VMEM((V7X_SUBLANES, width), F32), pltpu.VMEM((V7X_SUBLANES, width), F32)],
        compiler_params=_cparams(("parallel", "arbitrary")),
    )(gate, rec, cw, pv, wr, wi)


def _proj_norm_route_body(*refs, n_in):
    x_ref = refs[0]
    a_refs = refs[1:1 + n_in]
    w_refs = refs[1 + n_in:1 + 2 * n_in]
    g_ref, whl_ref, b_ref, slt_ref, xo_ref, h_ref, rt_ref, cnt_ref, run_ref = refs[1 + 2 * n_in:]

    @pl.when(pl.program_id(0) == 0)
    def _():
        run_ref[...] = jnp.zeros_like(run_ref)

    x = x_ref[...]
    for a_ref, w_ref in zip(a_refs, w_refs):
        x = x + jnp.dot(a_ref[...], w_ref[...], preferred_element_type=F32)
    xo_ref[...] = x
    ms = jnp.mean(x * x, axis=-1, keepdims=True)
    h = x * lax.rsqrt(ms + RMS_EPS) * g_ref[...]
    h_hi, h_lo = _split_hilo(h)
    h_ref[...] = h_hi
    both = jnp.dot(h_hi, whl_ref[...], preferred_element_type=F32)
    lg = (both[:, :V7X_LANES] + both[:, V7X_LANES:]
          + jnp.dot(h_lo, whl_ref[:, :V7X_LANES], preferred_element_type=F32)) + b_ref[0:1]
    lane = lax.broadcasted_iota(jnp.int32, lg.shape, 1)
    neg_inf = -jnp.inf
    is_g = lane < N_GROUPS
    gl = jnp.where(is_g, lg, neg_inf)
    gmax = jnp.max(gl, axis=-1, keepdims=True)
    gidx = jnp.min(jnp.where(gl == gmax, lane, V7X_LANES), axis=-1, keepdims=True)
    g_top = 1.0 / jnp.sum(jnp.where(is_g, jnp.exp(gl - gmax), 0.0), axis=-1, keepdims=True)
    lo_e = N_GROUPS + EXPERTS_PER_GROUP * gidx
    el = jnp.where((lane >= lo_e) & (lane < lo_e + EXPERTS_PER_GROUP), lg, neg_inf)
    m1 = jnp.max(el, axis=-1, keepdims=True)
    i1 = jnp.min(jnp.where(el == m1, lane, V7X_LANES), axis=-1, keepdims=True)
    el2 = jnp.where(lane == i1, neg_inf, el)
    m2 = jnp.max(el2, axis=-1, keepdims=True)
    i2 = jnp.min(jnp.where(el2 == m2, lane, V7X_LANES), axis=-1, keepdims=True)
    t = jnp.exp(m2 - m1)
    gate1 = g_top / (1.0 + t)
    gate2 = g_top * t / (1.0 + t)
    oh1 = lane == i1
    oh2 = lane == i2
    oh = jnp.where(oh1 | oh2, 1.0, 0.0)
    before = jnp.dot(slt_ref[...], oh.astype(BF16), preferred_element_type=F32) + run_ref[0:1]
    rank1 = jnp.sum(jnp.where(oh1, before, 0.0), axis=-1, keepdims=True)
    rank2 = jnp.sum(jnp.where(oh2, before, 0.0), axis=-1, keepdims=True)
    run_ref[0:1] = run_ref[0:1] + jnp.sum(oh, axis=0, keepdims=True)
    cnt_ref[...] = jnp.broadcast_to(run_ref[0:1], cnt_ref.shape)
    vals = (i1 - N_GROUPS, i2 - N_GROUPS, rank1, rank2, gate1, gate2)
    out = jnp.zeros(lg.shape, F32)
    for li, val in enumerate(vals):
        out = jnp.where(lane == li, val.astype(F32), out)
    rt_ref[...] = out.T[0:V7X_SUBLANES]


def _proj_norm_route(x, acts, ws, g, w_router, b_router, tm):
    n, d = x.shape
    n_in = len(acts)
    nr = w_router.shape[1]
    wpad = jnp.zeros((d, V7X_LANES), F32).at[:, :nr].set(w_router)
    wh = wpad.astype(BF16)
    whl = jnp.concatenate([wh, (wpad - wh.astype(F32)).astype(BF16)], axis=1)
    bias = jnp.zeros((V7X_SUBLANES, V7X_LANES), F32).at[0, :nr].set(b_router)
    slt = jnp.asarray(np.tril(np.ones((tm, tm), np.float32), -1), BF16)
    const = lambda shape: pl.BlockSpec(shape, lambda i: (0, 0))
    rows = lambda width: pl.BlockSpec((tm, width), lambda i: (i, 0))
    return pl.pallas_call(
        functools.partial(_proj_norm_route_body, n_in=n_in),
        grid=(n // tm,),
        in_specs=([rows(d)] + [rows(a.shape[1]) for a in acts] + [const(w.shape) for w in ws]
                  + [const((1, d)), const((d, 2 * V7X_LANES)),
                     const((V7X_SUBLANES, V7X_LANES)), const((tm, tm))]),
        out_specs=[rows(d), rows(d), pl.BlockSpec((V7X_SUBLANES, tm), lambda i: (0, i)),
                   const((V7X_SUBLANES, V7X_LANES))],
        out_shape=[jax.ShapeDtypeStruct((n, d), F32), jax.ShapeDtypeStruct((n, d), BF16),
                   jax.ShapeDtypeStruct((V7X_SUBLANES, n), F32),
                   jax.ShapeDtypeStruct((V7X_SUBLANES, V7X_LANES), F32)],
        scratch_shapes=[pltpu.VMEM((V7X_SUBLANES, V7X_LANES), F32)],
        compiler_params=_cparams(("arbitrary",)),
    )(x, *acts, *ws, g, whl, bias, slt)


def _moe_body(te_ref, nv_ref, x_ref, wg_ref, wu_ref, wd_ref, o_ref, wgb_ref, wub_ref, wdb_ref):
    i = pl.program_id(0)
    valid = i < nv_ref[0]

    @pl.when(valid & ((i == 0) | (te_ref[i] != te_ref[jnp.maximum(i - 1, 0)])))
    def _():
        wgb_ref[...] = wg_ref[0].astype(BF16)
        wub_ref[...] = wu_ref[0].astype(BF16)
        wdb_ref[...] = wd_ref[0].astype(BF16)

    @pl.when(jnp.logical_not(valid))
    def _():
        o_ref[...] = jnp.zeros_like(o_ref)

    @pl.when(valid)
    def _():
        x = x_ref[...]
        gt = jnp.dot(x, wgb_ref[...], preferred_element_type=F32)
        up = jnp.dot(x, wub_ref[...], preferred_element_type=F32)
        hid = (gt * _sigmoid(gt)) * up
        o_ref[...] = jnp.dot(hid.astype(BF16), wdb_ref[...], preferred_element_type=F32).astype(o_ref.dtype)


def _moe_grouped(xs, tile_expert, n_valid, wg, wu, wd, layer):
    p_rows, d = xs.shape
    de = wg.shape[3]
    tm = MOE_TILE
    grid_spec = pltpu.PrefetchScalarGridSpec(
        num_scalar_prefetch=2,
        grid=(p_rows // tm,),
        in_specs=[pl.BlockSpec((tm, d), lambda i, te, nv: (i, 0)),
                  pl.BlockSpec((None, 1, d, de), lambda i, te, nv: (layer, te[i], 0, 0)),
                  pl.BlockSpec((None, 1, d, de), lambda i, te, nv: (layer, te[i], 0, 0)),
                  pl.BlockSpec((None, 1, de, d), lambda i, te, nv: (layer, te[i], 0, 0))],
        out_specs=pl.BlockSpec((tm, d), lambda i, te, nv: (i, 0)),
        scratch_shapes=[pltpu.VMEM((d, de), BF16), pltpu.VMEM((d, de), BF16), pltpu.VMEM((de, d), BF16)],
    )
    return pl.pallas_call(
        _moe_body,
        grid_spec=grid_spec,
        out_shape=jax.ShapeDtypeStruct((p_rows, d), BF16),
        compiler_params=_cparams(("arbitrary",)),
    )(tile_expert, n_valid, xs, wg, wu, wd)


def _mix_out_moe(x, acts, ws, g_norm, w_group, b_group, w_erouter, b_erouter, wg, wu, wd, layer, tm):
    n, d = x.shape
    x, hn, route, cnt = _proj_norm_route(x, acts, ws, g_norm, jnp.concatenate([w_group, w_erouter], axis=1),
                                         jnp.concatenate([b_group, b_erouter]), tm)
    expert_id = route[0:2].astype(jnp.int32)
    rank = route[2:4].astype(jnp.int32)
    gates = route[4:6]
    counts = cnt[0, N_GROUPS:N_GROUPS + N_EXPERTS].astype(jnp.int32)

    tmm = MOE_TILE
    padded = ((counts + tmm - 1) // tmm) * tmm
    ends = jnp.cumsum(padded)
    starts = ends - padded
    start_of = jnp.zeros_like(expert_id)
    for e in range(N_EXPERTS):
        start_of = jnp.where(expert_id == e, starts[e], start_of)
    pos = start_of + rank
    p_rows = 2 * n + N_EXPERTS * tmm
    n_tiles = p_rows // tmm
    tok = jnp.broadcast_to(jnp.arange(n, dtype=jnp.int32)[None, :], (2, n))
    row_token = (jnp.arange(p_rows, dtype=jnp.int32) % n).at[pos.reshape(-1)].set(
        tok.reshape(-1), mode="promise_in_bounds", unique_indices=True)
    tile_expert = jnp.minimum(
        jnp.searchsorted(ends, jnp.arange(n_tiles, dtype=jnp.int32) * tmm, side="right"), N_EXPERTS - 1
    ).astype(jnp.int32)
    n_valid = (ends[-1] // tmm).astype(jnp.int32).reshape(1)

    take_rows = lambda a, idx: jnp.take(a, idx, axis=0, mode="clip")
    xs = take_rows(hn, row_token)
    ys = _moe_grouped(xs, tile_expert, n_valid, wg, wu, wd, layer)
    y = (gates[0][:, None] * take_rows(ys, pos[0]).astype(F32)
         + gates[1][:, None] * take_rows(ys, pos[1]).astype(F32))
    return x + y


def _pad_rows(w, rows, offset=0):
    out = jnp.zeros((rows, w.shape[1]), w.dtype)
    return out.at[offset:offset + w.shape[0]].set(w)


def kernel(x, norm_mix, norm_ffn, w_in_even, mu_a, w0, decay_up, a0, iclr_up, gate_up, k_k, k_a, r_k, gn_w,
           gn_b, q_norm_g, k_norm_g, w_out_even, w_in_odd, conv_w, conv_b, w_rgate, b_rgate, w_igate,
           b_igate, lru_lambda, w_out_odd, w_group, b_group, w_erouter, b_erouter, exp_w_gate, exp_w_up,
           exp_w_down):
    bsz, t_len, d = x.shape
    n = bsz * t_len
    depth = norm_mix.shape[0]
    aw = decay_up.shape[2]
    d_lora, i_lora, g_lora = decay_up.shape[1], iclr_up.shape[1], gate_up.shape[1]
    bw = (w_in_even.shape[2] - (3 * aw + d_lora + i_lora + g_lora)) // 3
    lw = w_rgate.shape[1] * w_rgate.shape[2]
    assert d_lora + i_lora == V7X_LANES and g_lora <= 2 * V7X_LANES
    lora_w = 3 * V7X_LANES
    tm = 512
    xf = x.reshape(n, d)

    for layer in range(depth):
        gm = norm_mix[layer].reshape(1, d)
        if layer % 2 == 0:
            i = layer // 2
            w_in = w_in_even[i]
            o3 = 3 * aw
            a_cols = o3 + d_lora + i_lora + g_lora
            pad = jnp.zeros((d, lora_w - (d_lora + i_lora + g_lora)), F32)
            w_cat = jnp.concatenate([w_in[:, :a_cols], pad, w_in[:, a_cols:]], axis=1).astype(BF16)
            n_rep = bw // HEAD_DIM
            head_gains = jnp.zeros((V7X_SUBLANES, bw), F32).at[0].set(
                jnp.tile(q_norm_g[i], n_rep) * (LOG2_E / math.sqrt(HEAD_DIM))).at[1].set(jnp.tile(k_norm_g[i], n_rep))
            r, k, v, lo, qb, kb, vb = _norm_matmul(xf, gm, w_cat, (aw, aw, aw, lora_w, bw, bw, bw), tm,
                                                   head_norm={4: 0, 5: 1}, head_gains=head_gains)
            mu = mu_a[i]
            rows = [mu[:aw], mu[aw:2 * aw], mu[2 * aw:o3], w0[i], a0[i], k_k[i], k_a[i], r_k[i].reshape(-1),
                    gn_w[i], gn_b[i]]
            prm = jnp.zeros((16, aw), F32).at[:len(rows)].set(jnp.stack(rows))
            mul = jnp.zeros((V7X_SUBLANES, lora_w), F32).at[0, :a_cols - o3].set(mu[o3:])
            du = _pad_rows(decay_up[i], V7X_LANES, 0).astype(BF16)
            iu = _pad_rows(iclr_up[i], V7X_LANES, d_lora).astype(BF16)
            gu = _pad_rows(gate_up[i], 2 * V7X_LANES, 0).astype(BF16)
            ya = _rwkv(r, k, v, lo, prm, mul, du, iu, gu, bsz, t_len)
            yb = _stick_breaking(qb, kb, vb, bsz, t_len)
            w_out = w_out_even[i].astype(BF16)
            acts, ws = [ya, yb], [w_out[:aw], w_out[aw:]]
        else:
            j = layer // 2
            gate, rec = _norm_matmul(xf, gm, w_in_odd[j].astype(BF16), (lw, lw), tm)
            cw = jnp.zeros((V7X_SUBLANES, lw), F32).at[:conv_w.shape[1]].set(conv_w[j])
            pv = jnp.zeros((V7X_SUBLANES, lw), F32).at[:4].set(
                jnp.stack([conv_b[j], b_rgate[j], b_igate[j], lru_lambda[j]]))
            yl = _rg_lru(gate, rec, cw, pv, w_rgate[j].astype(BF16), w_igate[j].astype(BF16), bsz, t_len,
                         min(256, t_len))
            acts, ws = [yl], [w_out_odd[j].astype(BF16)]
        xf = _mix_out_moe(xf, acts, ws, norm_ffn[layer].reshape(1, d), w_group[layer], b_group[layer],
                          w_erouter[layer], b_erouter[layer], exp_w_gate, exp_w_up, exp_w_down, layer, tm)
    return xf.reshape(bsz, t_len, d)
```

```python
import functools
import math

import numpy as np
import jax
import jax.numpy as jnp
from jax import lax
from jax.experimental import pallas as pl
from jax.experimental.pallas import tpu as pltpu

F32 = jnp.float32
BF16 = jnp.bfloat16

V7X_LANES = 128
V7X_SUBLANES = 8
V7X_VMEM_LIMIT_BYTES = 48 * 1024 * 1024

HEAD_DIM = 64
RMS_EPS = 1e-6
GN_EPS = 64e-5
LRU_C = 8.0
N_GROUPS = 4
EXPERTS_PER_GROUP = 4
N_EXPERTS = 16
RWKV_CHUNK = 128
RWKV_CHUNKS_PER_STEP = 2
SB_BLOCK = 128
SB_KEY_GROUP = 4
SB_QUERY_GROUP = 2
MOE_TILE = 512
NEG_BIG = -1e30
LOG2_E = 1.4426950408889634


def _cparams(sem):
    return pltpu.CompilerParams(dimension_semantics=sem, vmem_limit_bytes=V7X_VMEM_LIMIT_BYTES)


def _mm(a, b):
    return jnp.dot(a.astype(BF16), b.astype(BF16), preferred_element_type=F32)


def _mm_nt(a, b):
    return lax.dot_general(a.astype(BF16), b.astype(BF16), (((1,), (1,)), ((), ())),
                           preferred_element_type=F32)


def _split_hilo(x):
    hi = x.astype(BF16)
    lo = (x - hi.astype(F32)).astype(BF16)
    return hi, lo


def _mm_hilo(x, m):
    hi, lo = _split_hilo(x)
    return (jnp.dot(hi, m, preferred_element_type=F32) + jnp.dot(lo, m, preferred_element_type=F32))


def _hilo_mm(m, x):
    hi, lo = _split_hilo(x)
    return (jnp.dot(m, hi, preferred_element_type=F32) + jnp.dot(m, lo, preferred_element_type=F32))


def _sigmoid(x):
    return 0.5 * jnp.tanh(0.5 * x) + 0.5


def _neg_abs(x):
    bits = pltpu.bitcast(x, jnp.uint32) | jnp.uint32(0x80000000)
    return pltpu.bitcast(bits, F32)


def _softplus(x):
    return jnp.maximum(x, 0.0) + jnp.log(1.0 + jnp.exp(-jnp.abs(x)))


def _norm_matmul_body(x_ref, g_ref, w_ref, *rest, col_splits, head_norm):
    if head_norm:
        bones_ref, hg_ref = rest[:2]
        out_refs = rest[2:]
    else:
        out_refs = rest
    x = x_ref[...]
    ms = jnp.mean(x * x, axis=-1, keepdims=True)
    h = (x * lax.rsqrt(ms + RMS_EPS) * g_ref[...]).astype(BF16)
    off = 0
    for gi, (o_ref, c) in enumerate(zip(out_refs, col_splits)):
        y = jnp.dot(h, w_ref[:, off:off + c], preferred_element_type=F32)
        if gi in head_norm:
            row = head_norm[gi]
            msq = jnp.dot((y * y).astype(BF16), bones_ref[...], preferred_element_type=F32) * (1.0 / HEAD_DIM)
            y = y * lax.rsqrt(msq + RMS_EPS) * hg_ref[row:row + 1]
        o_ref[...] = y.astype(o_ref.dtype)
        off += c


def _norm_matmul(x, g, w, col_splits, tm, head_norm=None, head_gains=None):
    n, d = x.shape
    ctot = w.shape[1]
    assert sum(col_splits) == ctot and n % tm == 0
    head_norm = dict(head_norm or {})
    extra, extra_specs = [], []
    if head_norm:
        width = col_splits[next(iter(head_norm))]
        assert all(col_splits[gi] == width for gi in head_norm)
        heads = np.arange(width) // HEAD_DIM
        extra = [jnp.asarray((heads[:, None] == heads[None, :]).astype(np.float32), BF16), head_gains]
        extra_specs = [pl.BlockSpec((width, width), lambda i: (0, 0)),
                       pl.BlockSpec(head_gains.shape, lambda i: (0, 0))]
    return pl.pallas_call(
        functools.partial(_norm_matmul_body, col_splits=tuple(col_splits), head_norm=head_norm),
        grid=(n // tm,),
        in_specs=[pl.BlockSpec((tm, d), lambda i: (i, 0)),
                  pl.BlockSpec((1, d), lambda i: (0, 0)),
                  pl.BlockSpec((d, ctot), lambda i: (0, 0))] + extra_specs,
        out_specs=[pl.BlockSpec((tm, c), lambda i: (i, 0)) for c in col_splits],
        out_shape=[jax.ShapeDtypeStruct((n, c), BF16) for c in col_splits],
        compiler_params=_cparams(("parallel",)),
    )(x, g, w, *extra)


def _tri_inverse_many(mats, row, col):
    eye = (row == col).astype(F32)
    blk8 = (row // 8) == (col // 8)
    n1 = [jnp.where(blk8, a, 0.0) for a in mats]
    n2 = [_mm(n, n) for n in n1]
    ipn = [eye + n for n in n1]
    p = [i + _mm(i, m) for i, m in zip(ipn, n2)]
    n4 = [_mm(m, m) for m in n2]
    x = [q + _mm(q, m) for q, m in zip(p, n4)]
    k = 8
    c = mats[0].shape[0]
    while k < c:
        sel = ((row // (2 * k)) == (col // (2 * k))) & ((row // k) > (col // k))
        e = [jnp.where(sel, a, 0.0) for a in mats]
        xe = [_mm(xi, ei) for xi, ei in zip(x, e)]
        x = [xi + _mm(xei, xi) for xi, xei in zip(x, xe)]
        k *= 2
    return x


def _rwkv_body(r_ref, k_ref, v_ref, l_ref, p_ref, mul_ref, du_ref, iu_ref, gu_ref, bones_ref, lt_ref,
               o_ref, z_ref, carry_ref):
    c = RWKV_CHUNK
    rows = r_ref.shape[0]
    n_chunks = rows // c
    aw = r_ref.shape[1]
    lw = l_ref.shape[1]
    n_pairs = aw // V7X_LANES

    @pl.when(pl.program_id(1) == 0)
    def _():
        z_ref[...] = jnp.zeros_like(z_ref)
        carry_ref[...] = jnp.zeros_like(carry_ref)

    row1 = lax.broadcasted_iota(jnp.int32, (rows, 1), 0)

    def token_shift(raw, prev_last, mu):
        prev = jnp.where(row1 == 0, prev_last, pltpu.roll(raw, 1, 0))
        return raw + mu * (prev - raw)

    prm = p_ref[...]
    mu_r, mu_k, mu_v, w0, a0, k_k, k_a, r_k, gn_w, gn_b = (prm[i:i + 1] for i in range(10))

    r_raw = r_ref[...].astype(F32)
    k_raw = k_ref[...].astype(F32)
    v_raw = v_ref[...].astype(F32)
    l_raw = l_ref[...].astype(F32)
    r = token_shift(r_raw, carry_ref[0:1, 0:aw], mu_r)
    k = token_shift(k_raw, carry_ref[0:1, aw:2 * aw], mu_k)
    v = token_shift(v_raw, carry_ref[0:1, 2 * aw:3 * aw], mu_v)
    xl = token_shift(l_raw, carry_ref[0:1, 3 * aw:3 * aw + lw], mul_ref[0:1])
    carry_ref[0:1, 0:aw] = r_raw[rows - 1:rows]
    carry_ref[0:1, aw:2 * aw] = k_raw[rows - 1:rows]
    carry_ref[0:1, 2 * aw:3 * aw] = v_raw[rows - 1:rows]
    carry_ref[0:1, 3 * aw:3 * aw + lw] = l_raw[rows - 1:rows]

    l0 = xl[:, :V7X_LANES]
    l1 = xl[:, V7X_LANES:]
    dw = _mm(jnp.tanh(l0), du_ref[...])
    da = _mm(l0, iu_ref[...])
    g = _mm(_sigmoid(l1), gu_ref[...])
    w_log = -_softplus(-(w0 + dw)) - 0.5
    logw = -jnp.exp(w_log)
    a = _sigmoid(a0 + da)

    bones = bones_ref[...]

    def head_sums(t):
        return jnp.concatenate([_mm(t[:, p * V7X_LANES:(p + 1) * V7X_LANES], bones)
                                for p in range(n_pairs)], axis=1)

    kk = k * k_k
    ssq = head_sums(kk * kk)
    kk = kk * jnp.minimum(lax.rsqrt(ssq), 1e12)
    k_mod = k * (1.0 + (a - 1.0) * k_a)
    a_vec = -kk
    b_vec = kk * a

    cum_all = _hilo_mm(lt_ref[...], logw)

    row = lax.broadcasted_iota(jnp.int32, (c, c), 0)
    col = lax.broadcasted_iota(jnp.int32, (c, c), 1)
    strict = col < row
    incl = col <= row
    eye = row == col
    same_head = (row // HEAD_DIM) == (col // HEAD_DIM)
    m0 = lax.broadcasted_iota(jnp.int32, (1, V7X_LANES), 1) < HEAD_DIM

    def stack(x):
        return jnp.concatenate([jnp.where(m0, x, 0.0), jnp.where(m0, 0.0, x)], axis=0)

    a_s, a_t, r_s, r_t, b_s, k_s, b_p, k_p, p_end, v_c = ([] for _ in range(10))
    for j in range(n_chunks):
        rs = slice(j * c, (j + 1) * c)
        cum = cum_all[rs]
        cmid = cum[c // 2 - 1:c // 2]
        cend = cum[c - 1:c]
        emid = jnp.exp(cmid)
        a_s.append(a_vec[rs] * jnp.exp(cum - logw[rs] - cmid))
        a_t.append(a_s[j] * emid)
        r_s.append(r[rs] * jnp.exp(cum - cmid))
        r_t.append(r_s[j] * emid)
        e_neg = jnp.exp(cmid - cum)
        b_s.append(b_vec[rs] * e_neg)
        k_s.append(k_mod[rs] * e_neg)
        e_end = jnp.exp(cend - cum)
        b_p.append(b_vec[rs] * e_end)
        k_p.append(k_mod[rs] * e_end)
        p_end.append(jnp.exp(cend))
        v_c.append(v[rs])

    sls = [slice(p * V7X_LANES, (p + 1) * V7X_LANES) for p in range(n_pairs)]
    units = [(j, p) for j in range(n_chunks) for p in range(n_pairs)]
    nu = range(len(units))
    v_p = [v_c[j][:, sls[p]] for j, p in units]
    gm = [_mm_nt(jnp.concatenate([stack(a_s[j][:, sls[p]]), stack(r_s[j][:, sls[p]])], axis=0),
                 jnp.concatenate([b_s[j][:, sls[p]], k_s[j][:, sls[p]]], axis=0)) for j, p in units]
    aab = [jnp.where(strict, gm[u][h * c:(h + 1) * c, 0:c], 0.0) for u in nu for h in range(2)]
    tinv = _tri_inverse_many(aab, row, col)
    aak = [jnp.concatenate([jnp.where(strict, gm[u][h * c:(h + 1) * c, c:2 * c], 0.0) for h in range(2)],
                           axis=1) for u in nu]
    arb = [jnp.concatenate([jnp.where(incl, gm[u][(2 + h) * c:(3 + h) * c, 0:c], 0.0) for h in range(2)],
                           axis=1) for u in nu]
    ark = [jnp.concatenate([jnp.where(incl, gm[u][(2 + h) * c:(3 + h) * c, c:2 * c], 0.0) for h in range(2)],
                           axis=1) for u in nu]
    vs = [stack(x) for x in v_p]
    akv = [_mm(aak[u], vs[u]) for u in nu]
    wu = [_mm(jnp.concatenate([tinv[2 * u], tinv[2 * u + 1]], axis=1),
              jnp.concatenate([stack(a_t[j][:, sls[p]]), stack(akv[u])], axis=1))
          for u, (j, p) in enumerate(units)]
    w_t = [x[:, :V7X_LANES] for x in wu]
    u_v = [x[:, V7X_LANES:] for x in wu]
    q_t = [r_t[j][:, sls[p]] + _mm(arb[u], stack(w_t[u])) for u, (j, p) in enumerate(units)]
    y_v = [_mm(jnp.concatenate([arb[u], ark[u]], axis=1),
               jnp.concatenate([stack(u_v[u]), vs[u]], axis=0)) for u in nu]
    bp_t = [b_p[j][:, sls[p]].T for j, p in units]
    kp_t = [k_p[j][:, sls[p]].T for j, p in units]
    m_mat = [jnp.where(same_head, _mm(bp_t[u], w_t[u]), 0.0) for u in nu]
    n_mat = [jnp.where(same_head, _mm(jnp.concatenate([bp_t[u], kp_t[u]], axis=1),
                                      jnp.concatenate([u_v[u], v_p[u]], axis=0)), 0.0) for u in nu]
    p_col = [jnp.sum(jnp.where(eye, p_end[j][:, sls[p]], 0.0), axis=1, keepdims=True) for j, p in units]
    z = [z_ref[p] for p in range(n_pairs)]
    y_rows = []
    for j in range(n_chunks):
        y_parts = []
        for p in range(n_pairs):
            u = j * n_pairs + p
            y_parts.append(_mm(q_t[u], z[p]) + y_v[u])
            z[p] = p_col[u] * z[p] + _mm(m_mat[u], z[p]) + n_mat[u]
        y_rows.append(jnp.concatenate(y_parts, axis=1))
    for p in range(n_pairs):
        z_ref[p] = z[p]

    y = jnp.concatenate(y_rows, axis=0)
    inv_hd = 1.0 / HEAD_DIM
    mean = head_sums(y) * inv_hd
    d = y - mean
    var = head_sums(d * d) * inv_hd
    yn = d * lax.rsqrt(var + GN_EPS) * gn_w + gn_b
    bonus = head_sums(r * k_mod * r_k) * v
    o_ref[...] = ((yn + bonus) * g).astype(o_ref.dtype)


def _rwkv(r, k, v, l, prm, mul, du, iu, gu, bsz, t_len):
    n, aw = r.shape
    lw = l.shape[1]
    c = RWKV_CHUNK
    n_chunks = math.gcd(RWKV_CHUNKS_PER_STEP, t_len // c)
    rows = n_chunks * c
    nt = t_len // rows
    heads = np.arange(V7X_LANES) // HEAD_DIM
    bones = jnp.asarray((heads[:, None] == heads[None, :]).astype(np.float32), BF16)
    lt = jnp.asarray(np.kron(np.eye(n_chunks, dtype=np.float32), np.tril(np.ones((c, c), np.float32))), BF16)
    tok = lambda w: pl.BlockSpec((rows, w), lambda b, t: (b * nt + t, 0))
    full = lambda arr: pl.BlockSpec(arr.shape, lambda b, t: (0,) * arr.ndim)
    return pl.pallas_call(
        _rwkv_body,
        grid=(bsz, nt),
        in_specs=[tok(aw), tok(aw), tok(aw), tok(lw), full(prm), full(mul), full(du), full(iu), full(gu),
                  full(bones), full(lt)],
        out_specs=tok(aw),
        out_shape=jax.ShapeDtypeStruct((n, aw), BF16),
        scratch_shapes=[pltpu.VMEM((aw // V7X_LANES, V7X_LANES, V7X_LANES), F32),
                        pltpu.VMEM((V7X_SUBLANES, 3 * aw + lw), F32)],
        compiler_params=_cparams(("parallel", "arbitrary")),
    )(r, k, v, l, prm, mul, du, iu, gu, bones, lt)


def _sb_body(q_ref, k_ref, v_ref, su_ref, o_ref, acc_ref, car_ref):
    t_len = q_ref.shape[0]
    blk = SB_BLOCK
    m0 = lax.broadcasted_iota(jnp.int32, (1, V7X_LANES), 1) < HEAD_DIM
    head_mask = (m0, jnp.logical_not(m0))

    n_qblocks = t_len // blk
    chains = [(qq, h) for qq in range(SB_QUERY_GROUP) for h in range(2)]

    def process(q0, k0, n_kb, diag):
        width = n_kb * blk
        kb = k_ref[pl.ds(k0, width), :]
        vb = v_ref[pl.ds(k0, width), :]
        qb = [q_ref[pl.ds(q0 + qq * blk, blk), :] for qq in range(SB_QUERY_GROUP)]
        z = [_mm_nt(jnp.where(head_mask[h], qb[qq], jnp.zeros_like(qb[qq])), kb) for qq, h in chains]
        sp = [jnp.maximum(zc, 0.0) + jnp.log2(1.0 + jnp.exp2(_neg_abs(zc))) for zc in z]
        if diag:
            row = lax.broadcasted_iota(jnp.int32, (blk, width), 0)
            col = lax.broadcasted_iota(jnp.int32, (blk, width), 1)
            causal = [(col + k0) < (row + q0 + qq * blk) for qq in range(SB_QUERY_GROUP)]
            sp = [jnp.where(causal[qq], x, 0.0) for (qq, _), x in zip(chains, sp)]
        sp_lp = [x.astype(BF16) for x in sp]
        su = su_ref[...]
        rs = [[jnp.dot(x[:, b * blk:(b + 1) * blk], su, preferred_element_type=F32) for b in range(n_kb)]
              for x in sp_lp]
        att = []
        for c, (qq, _) in enumerate(chains):
            car = car_ref[c]
            after = [None] * n_kb
            for b in reversed(range(n_kb)):
                after[b] = rs[c][b][:, :blk] + car
                car = car + rs[c][b][:, blk:]
            car_ref[c] = car
            logit = z[c] - sp[c] - jnp.concatenate(after, axis=1)
            if diag:
                logit = jnp.where(causal[qq], logit, NEG_BIG)
            att.append(jnp.exp2(logit))
        for c in range(len(chains)):
            acc_ref[c] += _mm(att[c], vb)

    def sweep_pair(q0, n_wide, half_step):
        acc_ref[...] = jnp.zeros_like(acc_ref)
        car_ref[...] = jnp.zeros_like(car_ref)
        process(q0, q0, SB_QUERY_GROUP, True)
        base = q0
        if half_step:
            base = q0 - SB_QUERY_GROUP * blk
            process(q0, base, SB_QUERY_GROUP, False)
        wide = SB_KEY_GROUP * blk

        def kstep(jj, _):
            process(q0, pl.multiple_of(base - (jj + 1) * wide, wide), SB_KEY_GROUP, False)
            return 0

        lax.fori_loop(0, n_wide, kstep, 0)
        for qq in range(SB_QUERY_GROUP):
            o_ref[pl.ds(q0 + qq * blk, blk), :] = jnp.where(m0, acc_ref[2 * qq], acc_ref[2 * qq + 1]
                                                            ).astype(o_ref.dtype)

    quad = SB_KEY_GROUP * blk
    n_quads = n_qblocks // SB_KEY_GROUP

    def quad_step(m, _):
        q0 = pl.multiple_of(m * quad, quad)
        sweep_pair(q0, m, False)
        sweep_pair(q0 + SB_QUERY_GROUP * blk, m, True)
        return 0

    lax.fori_loop(0, n_quads, quad_step, 0)
    if n_qblocks % SB_KEY_GROUP:
        sweep_pair(n_quads * quad, n_quads, False)


def _stick_breaking(q, k, v, bsz, t_len):
    n, bw = q.shape
    n_pairs = bw // V7X_LANES
    blk = SB_BLOCK
    idx = np.arange(blk)
    su = np.concatenate([(idx[:, None] > idx[None, :]).astype(np.float32), np.ones((blk, blk), np.float32)], axis=1)
    su = jnp.asarray(su, BF16)
    tok = pl.BlockSpec((t_len, V7X_LANES), lambda b, p: (b, p))
    assert SB_KEY_GROUP == 2 * SB_QUERY_GROUP and t_len % (SB_QUERY_GROUP * blk) == 0
    qgroup = SB_QUERY_GROUP
    return pl.pallas_call(
        _sb_body,
        grid=(bsz, n_pairs),
        in_specs=[tok, tok, tok, pl.BlockSpec(su.shape, lambda b, p: (0, 0))],
        out_specs=tok,
        out_shape=jax.ShapeDtypeStruct((n, bw), BF16),
        scratch_shapes=[pltpu.VMEM((2 * qgroup, blk, V7X_LANES), F32),
                        pltpu.VMEM((2 * qgroup, blk, blk), F32)],
        compiler_params=_cparams(("parallel", "parallel")),
    )(q, k, v, su)


def _lru_body(gate_ref, rec_ref, cw_ref, pv_ref, wr_ref, wi_ref, o_ref, prev_ref, h_ref):
    tt = rec_ref.shape[0]
    width = rec_ref.shape[1]
    n_blocks = width // V7X_LANES

    @pl.when(pl.program_id(1) == 0)
    def _():
        prev_ref[...] = jnp.zeros_like(prev_ref)
        h_ref[...] = jnp.zeros_like(h_ref)

    row8 = lax.broadcasted_iota(jnp.int32, (V7X_SUBLANES, 1), 0)

    for gi in range(n_blocks):
        sl = slice(gi * V7X_LANES, (gi + 1) * V7X_LANES)
        u = rec_ref[:, sl].astype(F32)
        prev = prev_ref[:, sl]

        def shifted(s):
            ru = pltpu.roll(u, s, 0)
            first = jnp.where(row8 < s, pltpu.roll(prev, s, 0), ru[0:V7X_SUBLANES])
            return jnp.concatenate([first, ru[V7X_SUBLANES:]], axis=0)

        cw = cw_ref[:, sl]
        rc = (cw[3:4] * u + cw[2:3] * shifted(1) + cw[1:2] * shifted(2) + cw[0:1] * shifted(3)
              + pv_ref[0:1, sl])
        prev_ref[:, sl] = u[tt - V7X_SUBLANES:tt]

        rgate = _sigmoid(_mm(rc, wr_ref[gi]) + pv_ref[1:2, sl])
        igate = _sigmoid(_mm(rc, wi_ref[gi]) + pv_ref[2:3, sl])
        log_a = -LRU_C * rgate * _softplus(-pv_ref[3:4, sl])
        a = jnp.exp(log_a)
        om = 1.0 - a * a
        b = jnp.where(om > 0.0, om * lax.rsqrt(om), 0.0) * (igate * rc)

        s = 1
        while s < V7X_SUBLANES:
            ra = pltpu.roll(a, s, 0)
            rb = pltpu.roll(b, s, 0)
            a_sh = jnp.concatenate([jnp.where(row8 >= s, ra[0:V7X_SUBLANES], 1.0), ra[V7X_SUBLANES:]], axis=0)
            b_sh = jnp.concatenate([jnp.where(row8 >= s, rb[0:V7X_SUBLANES], 0.0), rb[V7X_SUBLANES:]], axis=0)
            b = a * b_sh + b
            a = a * a_sh
            s *= 2
        n_groups = tt // V7X_SUBLANES
        ag = [a[j * V7X_SUBLANES:(j + 1) * V7X_SUBLANES] for j in range(n_groups)]
        bg = [b[j * V7X_SUBLANES:(j + 1) * V7X_SUBLANES] for j in range(n_groups)]
        k = 1
        while k < n_groups:
            bg = [bg[j] if j < k else ag[j] * bg[j - k] + bg[j] for j in range(n_groups)]
            ag = [ag[j] if j < k else ag[j] * ag[j - k] for j in range(n_groups)]
            k *= 2
        h_in = h_ref[0:1, sl]
        h = jnp.concatenate([ag[j] * h_in + bg[j] for j in range(n_groups)], axis=0)
        h_ref[0:1, sl] = h[tt - 1:tt]
        o_ref[:, sl] = (jax.nn.gelu(gate_ref[:, sl].astype(F32)) * h).astype(o_ref.dtype)


def _rg_lru(gate, rec, cw, pv, wr, wi, bsz, t_len, tt):
    n, width = rec.shape
    nt = t_len // tt
    tok = pl.BlockSpec((tt, width), lambda b, t: (b * nt + t, 0))
    full = lambda arr: pl.BlockSpec(arr.shape, lambda b, t: (0,) * arr.ndim)
    return pl.pallas_call(
        _lru_body,
        grid=(bsz, nt),
        in_specs=[tok, tok, full(cw), full(pv), full(wr), full(wi)],
        out_specs=tok,
        out_shape=jax.ShapeDtypeStruct((n, width), BF16),
        scratch_shapes=[pltpu.VMEM((V7X_SUBLANES, width), F32), pltpu.VMEM((V7X_SUBLANES, width), F32)],
        compiler_params=_cparams(("parallel", "arbitrary")),
    )(gate, rec, cw, pv, wr, wi)


def _proj_norm_route_body(*refs, n_in):
    x_ref = refs[0]
    a_refs = refs[1:1 + n_in]
    w_refs = refs[1 + n_in:1 + 2 * n_in]
    g_ref, whl_ref, b_ref, slt_ref, xo_ref, h_ref, rt_ref, cnt_ref, run_ref = refs[1 + 2 * n_in:]

    @pl.when(pl.program_id(0) == 0)
    def _():
        run_ref[...] = jnp.zeros_like(run_ref)

    x = x_ref[...]
    for a_ref, w_ref in zip(a_refs, w_refs):
        x = x + jnp.dot(a_ref[...], w_ref[...], preferred_element_type=F32)
    xo_ref[...] = x
    ms = jnp.mean(x * x, axis=-1, keepdims=True)
    h = x * lax.rsqrt(ms + RMS_EPS) * g_ref[...]
    h_hi, h_lo = _split_hilo(h)
    h_ref[...] = h_hi
    both = jnp.dot(h_hi, whl_ref[...], preferred_element_type=F32)
    lg = (both[:, :V7X_LANES] + both[:, V7X_LANES:]
          + jnp.dot(h_lo, whl_ref[:, :V7X_LANES], preferred_element_type=F32)) + b_ref[0:1]
    lane = lax.broadcasted_iota(jnp.int32, lg.shape, 1)
    neg_inf = -jnp.inf
    is_g = lane < N_GROUPS
    gl = jnp.where(is_g, lg, neg_inf)
    gmax = jnp.max(gl, axis=-1, keepdims=True)
    gidx = jnp.min(jnp.where(gl == gmax, lane, V7X_LANES), axis=-1, keepdims=True)
    g_top = 1.0 / jnp.sum(jnp.where(is_g, jnp.exp(gl - gmax), 0.0), axis=-1, keepdims=True)
    lo_e = N_GROUPS + EXPERTS_PER_GROUP * gidx
    el = jnp.where((lane >= lo_e) & (lane < lo_e + EXPERTS_PER_GROUP), lg, neg_inf)
    m1 = jnp.max(el, axis=-1, keepdims=True)
    i1 = jnp.min(jnp.where(el == m1, lane, V7X_LANES), axis=-1, keepdims=True)
    el2 = jnp.where(lane == i1, neg_inf, el)
    m2 = jnp.max(el2, axis=-1, keepdims=True)
    i2 = jnp.min(jnp.where(el2 == m2, lane, V7X_LANES), axis=-1, keepdims=True)
    t = jnp.exp(m2 - m1)
    gate1 = g_top / (1.0 + t)
    gate2 = g_top * t / (1.0 + t)
    oh1 = lane == i1
    oh2 = lane == i2
    oh = jnp.where(oh1 | oh2, 1.0, 0.0)
    before = jnp.dot(slt_ref[...], oh.astype(BF16), preferred_element_type=F32) + run_ref[0:1]
    rank1 = jnp.sum(jnp.where(oh1, before, 0.0), axis=-1, keepdims=True)
    rank2 = jnp.sum(jnp.where(oh2, before, 0.0), axis=-1, keepdims=True)
    run_ref[0:1] = run_ref[0:1] + jnp.sum(oh, axis=0, keepdims=True)
    cnt_ref[...] = jnp.broadcast_to(run_ref[0:1], cnt_ref.shape)
    vals = (i1 - N_GROUPS, i2 - N_GROUPS, rank1, rank2, gate1, gate2)
    out = jnp.zeros(lg.shape, F32)
    for li, val in enumerate(vals):
        out = jnp.where(lane == li, val.astype(F32), out)
    rt_ref[...] = out.T[0:V7X_SUBLANES]


def _proj_norm_route(x, acts, ws, g, w_router, b_router, tm):
    n, d = x.shape
    n_in = len(acts)
    nr = w_router.shape[1]
    wpad = jnp.zeros((d, V7X_LANES), F32).at[:, :nr].set(w_router)
    wh = wpad.astype(BF16)
    whl = jnp.concatenate([wh, (wpad - wh.astype(F32)).astype(BF16)], axis=1)
    bias = jnp.zeros((V7X_SUBLANES, V7X_LANES), F32).at[0, :nr].set(b_router)
    slt = jnp.asarray(np.tril(np.ones((tm, tm), np.float32), -1), BF16)
    const = lambda shape: pl.BlockSpec(shape, lambda i: (0, 0))
    rows = lambda width: pl.BlockSpec((tm, width), lambda i: (i, 0))
    return pl.pallas_call(
        functools.partial(_proj_norm_route_body, n_in=n_in),
        grid=(n // tm,),
        in_specs=([rows(d)] + [rows(a.shape[1]) for a in acts] + [const(w.shape) for w in ws]
                  + [const((1, d)), const((d, 2 * V7X_LANES)),
                     const((V7X_SUBLANES, V7X_LANES)), const((tm, tm))]),
        out_specs=[rows(d), rows(d), pl.BlockSpec((V7X_SUBLANES, tm), lambda i: (0, i)),
                   const((V7X_SUBLANES, V7X_LANES))],
        out_shape=[jax.ShapeDtypeStruct((n, d), F32), jax.ShapeDtypeStruct((n, d), BF16),
                   jax.ShapeDtypeStruct((V7X_SUBLANES, n), F32),
                   jax.ShapeDtypeStruct((V7X_SUBLANES, V7X_LANES), F32)],
        scratch_shapes=[pltpu.VMEM((V7X_SUBLANES, V7X_LANES), F32)],
        compiler_params=_cparams(("arbitrary",)),
    )(x, *acts, *ws, g, whl, bias, slt)


def _moe_body(te_ref, nv_ref, x_ref, wg_ref, wu_ref, wd_ref, o_ref, wgb_ref, wub_ref, wdb_ref):
    i = pl.program_id(0)
    valid = i < nv_ref[0]

    @pl.when(valid & ((i == 0) | (te_ref[i] != te_ref[jnp.maximum(i - 1, 0)])))
    def _():
        wgb_ref[...] = wg_ref[0].astype(BF16)
        wub_ref[...] = wu_ref[0].astype(BF16)
        wdb_ref[...] = wd_ref[0].astype(BF16)

    @pl.when(jnp.logical_not(valid))
    def _():
        o_ref[...] = jnp.zeros_like(o_ref)

    @pl.when(valid)
    def _():
        x = x_ref[...]
        gt = jnp.dot(x, wgb_ref[...], preferred_element_type=F32)
        up = jnp.dot(x, wub_ref[...], preferred_element_type=F32)
        hid = (gt * _sigmoid(gt)) * up
        o_ref[...] = jnp.dot(hid.astype(BF16), wdb_ref[...], preferred_element_type=F32).astype(o_ref.dtype)


def _moe_grouped(xs, tile_expert, n_valid, wg, wu, wd, layer):
    p_rows, d = xs.shape
    de = wg.shape[3]
    tm = MOE_TILE
    grid_spec = pltpu.PrefetchScalarGridSpec(
        num_scalar_prefetch=2,
        grid=(p_rows // tm,),
        in_specs=[pl.BlockSpec((tm, d), lambda i, te, nv: (i, 0)),
                  pl.BlockSpec((None, 1, d, de), lambda i, te, nv: (layer, te[i], 0, 0)),
                  pl.BlockSpec((None, 1, d, de), lambda i, te, nv: (layer, te[i], 0, 0)),
                  pl.BlockSpec((None, 1, de, d), lambda i, te, nv: (layer, te[i], 0, 0))],
        out_specs=pl.BlockSpec((tm, d), lambda i, te, nv: (i, 0)),
        scratch_shapes=[pltpu.VMEM((d, de), BF16), pltpu.VMEM((d, de), BF16), pltpu.VMEM((de, d), BF16)],
    )
    return pl.pallas_call(
        _moe_body,
        grid_spec=grid_spec,
        out_shape=jax.ShapeDtypeStruct((p_rows, d), BF16),
        compiler_params=_cparams(("arbitrary",)),
    )(tile_expert, n_valid, xs, wg, wu, wd)


def _mix_out_moe(x, acts, ws, g_norm, w_group, b_group, w_erouter, b_erouter, wg, wu, wd, layer, tm):
    n, d = x.shape
    x, hn, route, cnt = _proj_norm_route(x, acts, ws, g_norm, jnp.concatenate([w_group, w_erouter], axis=1),
                                         jnp.concatenate([b_group, b_erouter]), tm)
    expert_id = route[0:2].astype(jnp.int32)
    rank = route[2:4].astype(jnp.int32)
    gates = route[4:6]
    counts = cnt[0, N_GROUPS:N_GROUPS + N_EXPERTS].astype(jnp.int32)

    tmm = MOE_TILE
    padded = ((counts + tmm - 1) // tmm) * tmm
    ends = jnp.cumsum(padded)
    starts = ends - padded
    start_of = jnp.zeros_like(expert_id)
    for e in range(N_EXPERTS):
        start_of = jnp.where(expert_id == e, starts[e], start_of)
    pos = start_of + rank
    p_rows = 2 * n + N_EXPERTS * tmm
    n_tiles = p_rows // tmm
    tok = jnp.broadcast_to(jnp.arange(n, dtype=jnp.int32)[None, :], (2, n))
    row_token = (jnp.arange(p_rows, dtype=jnp.int32) % n).at[pos.reshape(-1)].set(
        tok.reshape(-1), mode="promise_in_bounds", unique_indices=True)
    tile_first = jnp.arange(n_tiles, dtype=jnp.int32) * tmm
    tile_expert = jnp.minimum(jnp.sum((ends[None, :] <= tile_first[:, None]).astype(jnp.int32), axis=1),
                              N_EXPERTS - 1).astype(jnp.int32)
    n_valid = (ends[-1] // tmm).astype(jnp.int32).reshape(1)

    take_rows = lambda a, idx: jnp.take(a, idx, axis=0, mode="clip")
    xs = take_rows(hn, row_token)
    ys = _moe_grouped(xs, tile_expert, n_valid, wg, wu, wd, layer)
    y = (gates[0][:, None] * take_rows(ys, pos[0]).astype(F32)
         + gates[1][:, None] * take_rows(ys, pos[1]).astype(F32))
    return x + y


def _pad_rows(w, rows, offset=0):
    out = jnp.zeros((rows, w.shape[1]), w.dtype)
    return out.at[offset:offset + w.shape[0]].set(w)


def kernel(x, norm_mix, norm_ffn, w_in_even, mu_a, w0, decay_up, a0, iclr_up, gate_up, k_k, k_a, r_k, gn_w,
           gn_b, q_norm_g, k_norm_g, w_out_even, w_in_odd, conv_w, conv_b, w_rgate, b_rgate, w_igate,
           b_igate, lru_lambda, w_out_odd, w_group, b_group, w_erouter, b_erouter, exp_w_gate, exp_w_up,
           exp_w_down):
    bsz, t_len, d = x.shape
    n = bsz * t_len
    depth = norm_mix.shape[0]
    aw = decay_up.shape[2]
    d_lora, i_lora, g_lora = decay_up.shape[1], iclr_up.shape[1], gate_up.shape[1]
    bw = (w_in_even.shape[2] - (3 * aw + d_lora + i_lora + g_lora)) // 3
    lw = w_rgate.shape[1] * w_rgate.shape[2]
    assert d_lora + i_lora == V7X_LANES and g_lora <= 2 * V7X_LANES
    lora_w = 3 * V7X_LANES
    tm = 512
    xf = x.reshape(n, d)

    for layer in range(depth):
        gm = norm_mix[layer].reshape(1, d)
        if layer % 2 == 0:
            i = layer // 2
            w_in = w_in_even[i]
            o3 = 3 * aw
            a_cols = o3 + d_lora + i_lora + g_lora
            pad = jnp.zeros((d, lora_w - (d_lora + i_lora + g_lora)), F32)
            w_cat = jnp.concatenate([w_in[:, :a_cols], pad, w_in[:, a_cols:]], axis=1).astype(BF16)
            n_rep = bw // HEAD_DIM
            head_gains = jnp.zeros((V7X_SUBLANES, bw), F32).at[0].set(
                jnp.tile(q_norm_g[i], n_rep) * (LOG2_E / math.sqrt(HEAD_DIM))).at[1].set(jnp.tile(k_norm_g[i], n_rep))
            r, k, v, lo, qb, kb, vb = _norm_matmul(xf, gm, w_cat, (aw, aw, aw, lora_w, bw, bw, bw), tm,
                                                   head_norm={4: 0, 5: 1}, head_gains=head_gains)
            mu = mu_a[i]
            rows = [mu[:aw], mu[aw:2 * aw], mu[2 * aw:o3], w0[i], a0[i], k_k[i], k_a[i], r_k[i].reshape(-1),
                    gn_w[i], gn_b[i]]
            prm = jnp.zeros((16, aw), F32).at[:len(rows)].set(jnp.stack(rows))
            mul = jnp.zeros((V7X_SUBLANES, lora_w), F32).at[0, :a_cols - o3].set(mu[o3:])
            du = _pad_rows(decay_up[i], V7X_LANES, 0).astype(BF16)
            iu = _pad_rows(iclr_up[i], V7X_LANES, d_lora).astype(BF16)
            gu = _pad_rows(gate_up[i], 2 * V7X_LANES, 0).astype(BF16)
            ya = _rwkv(r, k, v, lo, prm, mul, du, iu, gu, bsz, t_len)
            yb = _stick_breaking(qb, kb, vb, bsz, t_len)
            w_out = w_out_even[i].astype(BF16)
            acts, ws = [ya, yb], [w_out[:aw], w_out[aw:]]
        else:
            j = layer // 2
            gate, rec = _norm_matmul(xf, gm, w_in_odd[j].astype(BF16), (lw, lw), tm)
            cw = jnp.zeros((V7X_SUBLANES, lw), F32).at[:conv_w.shape[1]].set(conv_w[j])
            pv = jnp.zeros((V7X_SUBLANES, lw), F32).at[:4].set(
                jnp.stack([conv_b[j], b_rgate[j], b_igate[j], lru_lambda[j]]))
            yl = _rg_lru(gate, rec, cw, pv, w_rgate[j].astype(BF16), w_igate[j].astype(BF16), bsz, t_len,
                         min(256, t_len))
            acts, ws = [yl], [w_out_odd[j].astype(BF16)]
        xf = _mix_out_moe(xf, acts, ws, norm_ffn[layer].reshape(1, d), w_group[layer], b_group[layer],
                          w_erouter[layer], b_erouter[layer], exp_w_gate, exp_w_up, exp_w_down, layer, tm)
    return xf.reshape(bsz, t_len, d)
```
